```python
import math
import jax, jax.numpy as jnp
from jax import lax
import numpy as np

D_MODEL = 2048
BATCH = 2
SEQ = 8192
DEPTH = 2

GRID_W = 64
CTX_LEN = 256
HEAD_DIM = 128
A_Q_HEADS = D_MODEL // HEAD_DIM
A_KV_HEADS = A_Q_HEADS // 4
A_GROUP = A_Q_HEADS // A_KV_HEADS
WINDOW = 128
BLOCK = 128
B_QK_DIM = HEAD_DIM
B_V_DIM = 2 * B_QK_DIM
B_HEADS = D_MODEL // B_V_DIM
A_WIDTH = A_Q_HEADS * HEAD_DIM
A_KV_WIDTH = A_KV_HEADS * HEAD_DIM
B_QK_WIDTH = B_HEADS * 2 * B_QK_DIM
B_WIDTH = B_HEADS * B_V_DIM
KV_SPLITS = (A_KV_WIDTH, A_KV_WIDTH, B_QK_WIDTH, B_WIDTH)
Q_SPLITS = (A_WIDTH, A_WIDTH, B_QK_WIDTH, B_WIDTH, D_MODEL, D_MODEL)
KV_COLS = sum(KV_SPLITS)
Q_COLS = sum(Q_SPLITS)
IN_COLS = KV_COLS + Q_COLS
ROPE_THETA = 10000.0
EPS = 1e-6
NEG = -1e30
ADA_STD = 0.5

kernel_name = "hybrid_gated_window_gqa_diff_attn_dit"


def rmsnorm(x, g):
    xf = x.astype(jnp.float32)
    y = xf * lax.rsqrt(jnp.mean(xf * xf, axis=-1, keepdims=True) + EPS)
    return (y * g.astype(jnp.float32)).astype(x.dtype)


def split_cols(p, sizes):
    idx = np.cumsum(sizes)[:-1].tolist()
    return jnp.split(p, idx, axis=-1)


def axial_rope_tables(n_lat):
    rows = n_lat // GRID_W
    r = jnp.repeat(jnp.arange(rows, dtype=jnp.float32), GRID_W)
    col = jnp.tile(jnp.arange(GRID_W, dtype=jnp.float32), rows)
    n_freq = HEAD_DIM // 4
    inv = ROPE_THETA ** (-jnp.arange(n_freq, dtype=jnp.float32) / n_freq)
    ang = jnp.concatenate([r[:, None] * inv, col[:, None] * inv], axis=-1)
    return jnp.cos(ang), jnp.sin(ang)


def apply_rope(x, cos, sin):
    half = x.shape[-1] // 2
    x1, x2 = x[..., :half], x[..., half:]
    c = cos[None, :, None, :].astype(x.dtype)
    s = sin[None, :, None, :].astype(x.dtype)
    return jnp.concatenate([x1 * c - x2 * s, x2 * c + x1 * s], axis=-1)


def adaln(cvec, w, b):
    m = jax.nn.silu(cvec) @ w + b
    return jnp.split(m, 3, axis=-1)


def window_gqa_latent(q, k, v, k_ctx, v_ctx, sink):
    B, S = q.shape[0], q.shape[1]
    C = k_ctx.shape[1]
    nb = S // BLOCK
    scale = HEAD_DIM ** -0.5
    qb = q.reshape(B, nb, BLOCK, A_KV_HEADS, A_GROUP, HEAD_DIM) * scale
    pad = ((0, 0), (BLOCK, BLOCK), (0, 0), (0, 0))
    kb = jnp.pad(k, pad).reshape(B, nb + 2, BLOCK, A_KV_HEADS, HEAD_DIM)
    vb = jnp.pad(v, pad).reshape(B, nb + 2, BLOCK, A_KV_HEADS, HEAD_DIM)
    kw = jnp.concatenate([kb[:, :-2], kb[:, 1:-1], kb[:, 2:]], axis=2)
    vw = jnp.concatenate([vb[:, :-2], vb[:, 1:-1], vb[:, 2:]], axis=2)
    s_loc = jnp.einsum('bnqhgd,bnkhd->bnhgqk', qb, kw).astype(jnp.float32)
    blk = jnp.arange(nb)[:, None, None]
    qpos = blk * BLOCK + jnp.arange(BLOCK)[None, :, None]
    kpos = (blk - 1) * BLOCK + jnp.arange(3 * BLOCK)[None, None, :]
    valid = (kpos >= 0) & (kpos < S) & (jnp.abs(qpos - kpos) <= WINDOW)
    s_loc = jnp.where(valid[None, :, None, None], s_loc, NEG)
    s_ctx = jnp.einsum('bnqhgd,bchd->bnhgqc', qb, k_ctx).astype(jnp.float32)
    s_sink = jnp.broadcast_to(sink.astype(jnp.float32).reshape(1, 1, A_KV_HEADS, A_GROUP, 1, 1),
                              s_ctx.shape[:-1] + (1,))
    p = jax.nn.softmax(jnp.concatenate([s_loc, s_ctx, s_sink], axis=-1), axis=-1)
    p_loc = p[..., :3 * BLOCK].astype(v.dtype)
    p_ctx = p[..., 3 * BLOCK:3 * BLOCK + C].astype(v.dtype)
    o = (jnp.einsum('bnhgqk,bnkhd->bnqhgd', p_loc, vw)
         + jnp.einsum('bnhgqc,bchd->bnqhgd', p_ctx, v_ctx))
    return o.reshape(B, S, A_WIDTH)


def window_gqa_context(q, k, v, sink):
    B, C = q.shape[0], q.shape[1]
    qg = q.reshape(B, C, A_KV_HEADS, A_GROUP, HEAD_DIM) * (HEAD_DIM ** -0.5)
    s = jnp.einsum('bqhgd,bkhd->bhgqk', qg, k).astype(jnp.float32)
    s_sink = jnp.broadcast_to(sink.astype(jnp.float32).reshape(1, A_KV_HEADS, A_GROUP, 1, 1),
                              s.shape[:-1] + (1,))
    p = jax.nn.softmax(jnp.concatenate([s, s_sink], axis=-1), axis=-1)[..., :C].astype(v.dtype)
    o = jnp.einsum('bhgqk,bkhd->bqhgd', p, v)
    return o.reshape(B, C, A_WIDTH)


def diff_attn_latent(q, k_all, v_all, lam):
    B, S = q.shape[0], q.shape[1]
    nb = S // BLOCK
    qb = jnp.moveaxis(q.reshape(B, nb, BLOCK, B_HEADS, 2, B_QK_DIM) * (B_QK_DIM ** -0.5), 1, 0)

    def one_block(qblk):
        s = jnp.einsum('bqhmd,bkhmd->bhmqk', qblk, k_all).astype(jnp.float32)
        p = jax.nn.softmax(s, axis=-1)
        pd = (p[:, :, 0] - lam * p[:, :, 1]).astype(v_all.dtype)
        return jnp.einsum('bhqk,bkhe->bqhe', pd, v_all)

    o = lax.map(one_block, qb)
    return jnp.moveaxis(o, 0, 1).reshape(B, S, B_HEADS, B_V_DIM)


def diff_attn_context(q, k, v, lam):
    s = jnp.einsum('bqhmd,bkhmd->bhmqk', q * (B_QK_DIM ** -0.5), k).astype(jnp.float32)
    p = jax.nn.softmax(s, axis=-1)
    pd = (p[:, :, 0] - lam * p[:, :, 1]).astype(v.dtype)
    return jnp.einsum('bhqk,bkhe->bqhe', pd, v)


def diff_output(o, g_sub, lam_init):
    B, L = o.shape[0], o.shape[1]
    return (rmsnorm(o, g_sub) * (1.0 - lam_init)).reshape(B, L, B_WIDTH)


def branch_merge(o_a, z_a, o_b, z_b, g_a, g_b, wpa, wpb, wo):
    y_a = (o_a * jax.nn.silu(z_a)) @ wpa
    y_b = (o_b * jax.nn.silu(z_b)) @ wpb
    return (jax.nn.sigmoid(g_a) * y_a + jax.nn.sigmoid(g_b) * y_b) @ wo


def setup_inputs(seed: int = 0) -> dict:
    key = jax.random.key(seed)
    ks = jax.random.split(key, 16)
    f32 = jnp.float32
    D = D_MODEL
    nrm = lambda k, shape, s: jax.random.normal(k, shape, f32) * s
    return {
        "x": nrm(ks[0], (BATCH, SEQ, D), 1.0),
        "c": nrm(ks[1], (BATCH, D), 1.0),
        "ctx": nrm(ks[2], (BATCH, CTX_LEN, D), 1.0),
        "c_ctx": nrm(ks[3], (D,), 1.0),
        "w_ada": nrm(ks[4], (DEPTH, D, 3 * D), ADA_STD * D ** -0.5),
        "b_ada": nrm(ks[5], (DEPTH, 3 * D), 0.01),
        "g_pre": 1.0 + nrm(ks[6], (DEPTH, D), 0.02),
        "g_post": 1.0 + nrm(ks[7], (DEPTH, D), 0.02),
        "w_in": nrm(ks[8], (DEPTH, D, IN_COLS), D ** -0.5),
        "sink": nrm(ks[9], (DEPTH, A_Q_HEADS), 1.0),
        "lam_qk": nrm(ks[10], (DEPTH, 4, B_QK_DIM), 0.1),
        "g_subln": 1.0 + nrm(ks[11], (DEPTH, B_V_DIM), 0.02),
        "w_proj_a": nrm(ks[12], (DEPTH, A_WIDTH, D), A_WIDTH ** -0.5),
        "w_proj_b": nrm(ks[13], (DEPTH, B_WIDTH, D), B_WIDTH ** -0.5),
        "w_out": nrm(ks[14], (DEPTH, D, D), D ** -0.5),
    }


def reference(x, c, ctx, c_ctx, w_ada, b_ada, g_pre, g_post, w_in, sink, lam_qk, g_subln,
              w_proj_a, w_proj_b, w_out):
    B, S = x.shape[0], x.shape[1]
    C = ctx.shape[1]
    cos, sin = axial_rope_tables(S)
    for l in range(DEPTH):
        last = l == DEPTH - 1
        lam_init = 0.8 - 0.6 * math.exp(-0.3 * l)
        lq1, lk1, lq2, lk2 = [t.astype(jnp.float32) for t in lam_qk[l]]
        lam = jnp.exp(jnp.sum(lq1 * lk1)) - jnp.exp(jnp.sum(lq2 * lk2)) + lam_init

        sh_x, sc_x, gt_x = adaln(c, w_ada[l], b_ada[l])
        sh_c, sc_c, gt_c = adaln(c_ctx, w_ada[l], b_ada[l])
        hx = rmsnorm(x, g_pre[l]) * (1.0 + sc_x[:, None]) + sh_x[:, None]
        hc = rmsnorm(ctx, g_pre[l]) * (1.0 + sc_c) + sh_c

        px = hx @ w_in[l]
        k_a, v_a, k_b, v_b = split_cols(px[..., :KV_COLS], KV_SPLITS)
        q_a, z_a, q_b, z_b, g_a, g_b = split_cols(px[..., KV_COLS:], Q_SPLITS)
        q_a = apply_rope(q_a.reshape(B, S, A_Q_HEADS, HEAD_DIM), cos, sin)
        k_a = apply_rope(k_a.reshape(B, S, A_KV_HEADS, HEAD_DIM), cos, sin)
        v_a = v_a.reshape(B, S, A_KV_HEADS, HEAD_DIM)
        q_b = apply_rope(q_b.reshape(B, S, 2 * B_HEADS, B_QK_DIM), cos, sin).reshape(B, S, B_HEADS, 2, B_QK_DIM)
        k_b = apply_rope(k_b.reshape(B, S, 2 * B_HEADS, B_QK_DIM), cos, sin).reshape(B, S, B_HEADS, 2, B_QK_DIM)
        v_b = v_b.reshape(B, S, B_HEADS, B_V_DIM)

        pc_kv = hc @ w_in[l][:, :KV_COLS]
        kc_a, vc_a, kc_b, vc_b = split_cols(pc_kv, KV_SPLITS)
        kc_a = kc_a.reshape(B, C, A_KV_HEADS, HEAD_DIM)
        vc_a = vc_a.reshape(B, C, A_KV_HEADS, HEAD_DIM)
        kc_b = kc_b.reshape(B, C, B_HEADS, 2, B_QK_DIM)
        vc_b = vc_b.reshape(B, C, B_HEADS, B_V_DIM)

        o_a = window_gqa_latent(q_a, k_a, v_a, kc_a, vc_a, sink[l])
        k_all = jnp.concatenate([kc_b, k_b], axis=1)
        v_all = jnp.concatenate([vc_b, v_b], axis=1)
        o_b = diff_output(diff_attn_latent(q_b, k_all, v_all, lam), g_subln[l], lam_init)
        out_x = branch_merge(o_a, z_a, o_b, z_b, g_a, g_b, w_proj_a[l], w_proj_b[l], w_out[l])

        if not last:
            pc_q = hc @ w_in[l][:, KV_COLS:]
            qc_a, zc_a, qc_b, zc_b, gc_a, gc_b = split_cols(pc_q, Q_SPLITS)
            oc_a = window_gqa_context(qc_a.reshape(B, C, A_Q_HEADS, HEAD_DIM), kc_a, vc_a, sink[l])
            oc_b = diff_output(diff_attn_context(qc_b.reshape(B, C, B_HEADS, 2, B_QK_DIM), kc_b, vc_b, lam),
                               g_subln[l], lam_init)
            out_c = branch_merge(oc_a, zc_a, oc_b, zc_b, gc_a, gc_b, w_proj_a[l], w_proj_b[l], w_out[l])
            ctx = ctx + gt_c * rmsnorm(out_c, g_post[l])

        x = x + gt_x[:, None] * rmsnorm(out_x, g_post[l])
    return x
```

```python
import functools
import math

import jax
import jax.numpy as jnp
from jax import lax
from jax.experimental import pallas as pl
from jax.experimental.pallas import tpu as pltpu

HEAD_DIM = 128
GRID_W = 64
WINDOW = 128
A_GROUP = 4
ROPE_THETA = 10000.0
EPS = 1e-6
NEG = -1e30
Q_SCALE = HEAD_DIM ** -0.5
VMEM_LIMIT_BYTES = 56 * 1024 * 1024

F32 = jnp.float32
BF16 = jnp.bfloat16


def _cparams(*semantics):
    return pltpu.CompilerParams(dimension_semantics=semantics, vmem_limit_bytes=VMEM_LIMIT_BYTES)


def _pick(n, candidates):
    for c in candidates:
        if n % c == 0:
            return c
    raise ValueError(f"no tile in {candidates} divides {n}")


def _silu(v):
    return v * jax.nn.sigmoid(v)


def _ada_kernel(c_ref, w_ref, b_ref, o_ref):
    a = _silu(c_ref[...])
    o_ref[...] = jnp.dot(a, w_ref[...], preferred_element_type=F32,
                         precision=lax.Precision.HIGHEST) + b_ref[...]


def _ada(cvec8, w, b):
    d, n = w.shape
    tn = _pick(n, (1024, 768, 512, 384, 256, 128))
    return pl.pallas_call(
        _ada_kernel,
        out_shape=jax.ShapeDtypeStruct((8, n), F32),
        grid=(n // tn,),
        in_specs=[pl.BlockSpec((8, d), lambda j: (0, 0)),
                  pl.BlockSpec((d, tn), lambda j: (0, j)),
                  pl.BlockSpec((1, tn), lambda j: (0, j))],
        out_specs=pl.BlockSpec((8, tn), lambda j: (0, j)),
        compiler_params=_cparams("arbitrary"),
        name="ada_ln",
    )(cvec8, w, b.reshape(1, n))


def _prenorm_kernel(x_ref, g_ref, mod_ref, o_ref, *, d):
    x = x_ref[...]
    y = x * lax.rsqrt(jnp.mean(x * x, axis=-1, keepdims=True) + EPS)
    y = y * g_ref[...]
    shift = mod_ref[:, 0:d]
    scale = mod_ref[:, d:2 * d]
    o_ref[...] = (y * (1.0 + scale) + shift).astype(o_ref.dtype)


def _prenorm(x2, g, mod3, row_of_tile, tm):
    t, d = x2.shape
    return pl.pallas_call(
        functools.partial(_prenorm_kernel, d=d),
        out_shape=jax.ShapeDtypeStruct((t, d), BF16),
        grid=(t // tm,),
        in_specs=[pl.BlockSpec((tm, d), lambda i: (i, 0)),
                  pl.BlockSpec((1, d), lambda i: (0, 0)),
                  pl.BlockSpec((None, 1, 3 * d), lambda i: (row_of_tile(i), 0, 0))],
        out_specs=pl.BlockSpec((tm, d), lambda i: (i, 0)),
        compiler_params=_cparams("parallel"),
        name="prenorm",
    )(x2, g.reshape(1, d), mod3)


def _inproj_kernel(type_ref, h_ref, w_ref, cos_ref, sin_ref, o_ref, *, tile0):
    t = type_ref[tile0 + pl.program_id(1)]
    heads = o_ref.shape[1] // HEAD_DIM

    def matmul():
        return jnp.dot(h_ref[...], w_ref[...], preferred_element_type=F32)

    @pl.when(t == 0)
    def _():
        o_ref[...] = matmul().astype(o_ref.dtype)

    @pl.when(t != 0)
    def _():
        acc = matmul()
        rope = (t & 1).astype(F32)
        scale = jnp.where(t >= 2, Q_SCALE, 1.0).astype(F32)
        ca = (rope * cos_ref[...] + (1.0 - rope)) * scale
        sa = (rope * sin_ref[...]) * scale
        for hh in range(heads):
            a = acc[:, hh * HEAD_DIM:(hh + 1) * HEAD_DIM]
            r = a * ca + pltpu.roll(a, HEAD_DIM // 2, 1) * sa
            o_ref[:, hh * HEAD_DIM:(hh + 1) * HEAD_DIM] = r.astype(o_ref.dtype)


def _inproj(h, w, types, cosf, sinf, *, col0, ncols, tm, tn, pos_tiles):
    t, d = h.shape
    tile0 = col0 // tn
    grid_spec = pltpu.PrefetchScalarGridSpec(
        num_scalar_prefetch=1,
        grid=(t // tm, ncols // tn),
        in_specs=[pl.BlockSpec((tm, d), lambda i, j, ty: (i, 0)),
                  pl.BlockSpec((d, tn), lambda i, j, ty: (0, tile0 + j)),
                  pl.BlockSpec((tm, HEAD_DIM), lambda i, j, ty: (i % pos_tiles, 0)),
                  pl.BlockSpec((tm, HEAD_DIM), lambda i, j, ty: (i % pos_tiles, 0))],
        out_specs=pl.BlockSpec((tm, tn), lambda i, j, ty: (i, j)),
    )
    return pl.pallas_call(
        functools.partial(_inproj_kernel, tile0=tile0),
        out_shape=jax.ShapeDtypeStruct((t, ncols), BF16),
        grid_spec=grid_spec,
        compiler_params=_cparams("parallel", "arbitrary"),
        name="in_proj",
    )(types, h, w, cosf, sinf)


def _stack_heads(q):
    g = q.shape[1] // HEAD_DIM
    return jnp.concatenate([q[:, i * HEAD_DIM:(i + 1) * HEAD_DIM] for i in range(g)], axis=0)


def _unstack_heads(o, g):
    rows = o.shape[0] // g
    return jnp.concatenate([o[i * rows:(i + 1) * rows, :] for i in range(g)], axis=1)


def _sink_column(sink_ref, kvh, rows):
    return jnp.concatenate(
        [jnp.full((rows, 1), sink_ref[kvh * A_GROUP + g], F32) for g in range(A_GROUP)], axis=0)


def _qkt(q, k):
    return lax.dot_general(q, k, (((1,), (1,)), ((), ())), preferred_element_type=F32)


def _attn_a_kernel(sink_ref, q_ref, kp_ref, km_ref, kn_ref, vp_ref, vm_ref, vn_ref, kc_ref, vc_ref,
                   o_ref, *, seq):
    i = pl.program_id(1)
    kvh = pl.program_id(2)
    tq = q_ref.shape[0]
    k_win = jnp.concatenate([kp_ref[...], km_ref[...], kn_ref[...]], axis=0)
    v_win = jnp.concatenate([vp_ref[...], vm_ref[...], vn_ref[...]], axis=0)
    kc = kc_ref[...]
    vc = vc_ref[...]
    rows = A_GROUP * WINDOW
    sink = _sink_column(sink_ref, kvh, WINDOW)
    qi = lax.broadcasted_iota(jnp.int32, (rows, 3 * WINDOW), 0) & (WINDOW - 1)
    kj = lax.broadcasted_iota(jnp.int32, (rows, 3 * WINDOW), 1)
    band = jnp.abs(qi + WINDOW - kj) <= WINDOW
    for r in range(tq // WINDOW):
        q = _stack_heads(q_ref[r * WINDOW:(r + 1) * WINDOW, :])
        k = k_win[r * WINDOW:(r + 3) * WINDOW, :]
        v = v_win[r * WINDOW:(r + 3) * WINDOW, :]
        base = i * tq + (r - 1) * WINDOW
        valid = band & (kj >= -base) & (kj < seq - base)
        s_loc = jnp.where(valid, _qkt(q, k), NEG)
        s_ctx = _qkt(q, kc)
        m = jnp.maximum(jnp.maximum(jnp.max(s_loc, axis=-1, keepdims=True),
                                    jnp.max(s_ctx, axis=-1, keepdims=True)), sink)
        p_loc = jnp.exp(s_loc - m)
        p_ctx = jnp.exp(s_ctx - m)
        denom = (jnp.sum(p_loc, axis=-1, keepdims=True) + jnp.sum(p_ctx, axis=-1, keepdims=True)
                 + jnp.exp(sink - m))
        o = (jnp.dot(p_loc.astype(v.dtype), v, preferred_element_type=F32)
             + jnp.dot(p_ctx.astype(vc.dtype), vc, preferred_element_type=F32)) / denom
        o_ref[r * WINDOW:(r + 1) * WINDOW, :] = _unstack_heads(o, A_GROUP).astype(o_ref.dtype)


def _attn_a(pxl, pxc, sink, *, batch, seq, ctx_len, d, kv0_l, kv0_c, tq):
    kvh = d // HEAD_DIM // A_GROUP
    gw = A_GROUP * HEAD_DIM
    nq = seq // tq
    r = tq // WINDOW
    nblk = seq // WINDOW
    ka_l = (kv0_l + 2 * d) // HEAD_DIM
    va_l = ka_l + kvh
    ka_c = (kv0_c + 2 * d) // HEAD_DIM
    va_c = ka_c + kvh

    def prev(b, i, h):
        return (b * nblk + jnp.maximum(i * r - 1, 0), h)

    def nxt(b, i, h):
        return (b * nblk + jnp.minimum((i + 1) * r, nblk - 1), h)

    halo = (WINDOW, HEAD_DIM)
    return pl.pallas_call(
        functools.partial(_attn_a_kernel, seq=seq),
        out_shape=jax.ShapeDtypeStruct((batch * seq, d), BF16),
        grid=(batch, nq, kvh),
        in_specs=[
            pl.BlockSpec(memory_space=pltpu.SMEM),
            pl.BlockSpec((tq, gw), lambda b, i, h: (b * nq + i, h)),
            pl.BlockSpec(halo, lambda b, i, h: prev(b, i, ka_l + h)),
            pl.BlockSpec((tq, HEAD_DIM), lambda b, i, h: (b * nq + i, ka_l + h)),
            pl.BlockSpec(halo, lambda b, i, h: nxt(b, i, ka_l + h)),
            pl.BlockSpec(halo, lambda b, i, h: prev(b, i, va_l + h)),
            pl.BlockSpec((tq, HEAD_DIM), lambda b, i, h: (b * nq + i, va_l + h)),
            pl.BlockSpec(halo, lambda b, i, h: nxt(b, i, va_l + h)),
            pl.BlockSpec((ctx_len, HEAD_DIM), lambda b, i, h: (b, ka_c + h)),
            pl.BlockSpec((ctx_len, HEAD_DIM), lambda b, i, h: (b, va_c + h)),
        ],
        out_specs=pl.BlockSpec((tq, gw), lambda b, i, h: (b * nq + i, h)),
        compiler_params=_cparams("parallel", "parallel", "arbitrary"),
        name="attn_a_latent",
    )(sink, pxl, pxl, pxl, pxl, pxl, pxl, pxl, pxc, pxc)


def _attn_a_ctx_kernel(sink_ref, q_ref, k_ref, v_ref, o_ref):
    kvh = pl.program_id(1)
    rows = q_ref.shape[0]
    q = _stack_heads(q_ref[...])
    v = v_ref[...]
    sink = _sink_column(sink_ref, kvh, rows)
    s = _qkt(q, k_ref[...])
    m = jnp.maximum(jnp.max(s, axis=-1, keepdims=True), sink)
    p = jnp.exp(s - m)
    denom = jnp.sum(p, axis=-1, keepdims=True) + jnp.exp(sink - m)
    o = jnp.dot(p.astype(v.dtype), v, preferred_element_type=F32) / denom
    o_ref[...] = _unstack_heads(o, A_GROUP).astype(o_ref.dtype)


def _attn_a_ctx(pxc, sink, *, batch, ctx_len, d, kv0_c):
    kvh = d // HEAD_DIM // A_GROUP
    gw = A_GROUP * HEAD_DIM
    ka_c = (kv0_c + 2 * d) // HEAD_DIM
    va_c = ka_c + kvh
    return pl.pallas_call(
        _attn_a_ctx_kernel,
        out_shape=jax.ShapeDtypeStruct((batch * ctx_len, d), BF16),
        grid=(batch, kvh),
        in_specs=[pl.BlockSpec(memory_space=pltpu.SMEM),
                  pl.BlockSpec((ctx_len, gw), lambda b, h: (b, h)),
                  pl.BlockSpec((ctx_len, HEAD_DIM), lambda b, h: (b, ka_c + h)),
                  pl.BlockSpec((ctx_len, HEAD_DIM), lambda b, h: (b, va_c + h))],
        out_specs=pl.BlockSpec((ctx_len, gw), lambda b, h: (b, h)),
        compiler_params=_cparams("parallel", "arbitrary"),
        name="attn_a_context",
    )(sink, pxc, pxc, pxc)


def _attn_b_kernel(*refs, lam_init, has_ctx):
    if has_ctx:
        q_ref, k_ref, v_ref, kc_ref, vc_ref, lamqk_ref, g_ref, o_ref, m_scr, l_scr, acc_scr = refs
    else:
        q_ref, k_ref, v_ref, lamqk_ref, g_ref, o_ref, m_scr, l_scr, acc_scr = refs
    j = pl.program_id(3)
    nk = pl.num_programs(3)

    @pl.when(j == 0)
    def _():
        m_scr[...] = jnp.full(m_scr.shape, -jnp.inf, F32)
        l_scr[...] = jnp.zeros(l_scr.shape, F32)
        acc_scr[...] = jnp.zeros(acc_scr.shape, F32)

    def absorb(k, v):
        for mp in range(2):
            q = q_ref[:, mp * HEAD_DIM:(mp + 1) * HEAD_DIM]
            s = _qkt(q, k[:, mp * HEAD_DIM:(mp + 1) * HEAD_DIM])
            m_old = m_scr[mp]
            m_new = jnp.maximum(m_old, jnp.max(s, axis=-1, keepdims=True))
            alpha = jnp.exp(m_old - m_new)
            p = jnp.exp(s - m_new)
            l_scr[mp] = alpha * l_scr[mp] + jnp.sum(p, axis=-1, keepdims=True)
            acc_scr[mp] = alpha * acc_scr[mp] + jnp.dot(p.astype(v.dtype), v,
                                                        preferred_element_type=F32)
            m_scr[mp] = m_new

    if has_ctx:
        @pl.when(j == 0)
        def _():
            absorb(kc_ref[...], vc_ref[...])

    absorb(k_ref[...], v_ref[...])

    @pl.when(j == nk - 1)
    def _():
        lq = lamqk_ref[...]
        lam = (jnp.exp(jnp.sum(lq[0:1] * lq[1:2], axis=-1, keepdims=True))
               - jnp.exp(jnp.sum(lq[2:3] * lq[3:4], axis=-1, keepdims=True)) + lam_init)
        o = acc_scr[0] / l_scr[0] - lam * (acc_scr[1] / l_scr[1])
        y = o * lax.rsqrt(jnp.mean(o * o, axis=-1, keepdims=True) + EPS) * g_ref[...]
        o_ref[...] = (y * (1.0 - lam_init)).astype(o_ref.dtype)


def _attn_b(pxq, pxk, pxc, lamqk, g_sub, *, lam_init, batch, rows_q, rows_k, ctx_len, d,
            kv0_k, kv0_c, tq, tk, name):
    hw = 2 * HEAD_DIM
    heads = d // hw
    nq = rows_q // tq
    nk = rows_k // tk
    qb = (2 * d) // hw
    kb_k = kv0_k // hw
    vb_k = (kv0_k + d) // hw
    has_ctx = pxc is not None
    in_specs = [pl.BlockSpec((tq, hw), lambda b, h, i, j: (b * nq + i, qb + h)),
                pl.BlockSpec((tk, hw), lambda b, h, i, j: (b * nk + j, kb_k + h)),
                pl.BlockSpec((tk, hw), lambda b, h, i, j: (b * nk + j, vb_k + h))]
    args = [pxq, pxk, pxk]
    if has_ctx:
        kb_c = kv0_c // hw
        vb_c = (kv0_c + d) // hw
        in_specs += [pl.BlockSpec((ctx_len, hw), lambda b, h, i, j: (b, kb_c + h)),
                     pl.BlockSpec((ctx_len, hw), lambda b, h, i, j: (b, vb_c + h))]
        args += [pxc, pxc]
    in_specs += [pl.BlockSpec((4, HEAD_DIM), lambda b, h, i, j: (0, 0)),
                 pl.BlockSpec((1, hw), lambda b, h, i, j: (0, 0))]
    args += [lamqk, g_sub.reshape(1, hw)]
    return pl.pallas_call(
        functools.partial(_attn_b_kernel, lam_init=lam_init, has_ctx=has_ctx),
        out_shape=jax.ShapeDtypeStruct((batch * rows_q, d), BF16),
        grid=(batch, heads, nq, nk),
        in_specs=in_specs,
        out_specs=pl.BlockSpec((tq, hw), lambda b, h, i, j: (b * nq + i, h)),
        scratch_shapes=[pltpu.VMEM((2, tq, 1), F32), pltpu.VMEM((2, tq, 1), F32),
                        pltpu.VMEM((2, tq, hw), F32)],
        compiler_params=_cparams("parallel", "parallel", "parallel", "arbitrary"),
        name=name,
    )(*args)


def _merge1_kernel(oa_ref, za_ref, ob_ref, zb_ref, ga_ref, gb_ref, wa_ref, wb_ref, u_ref,
                   a_scr, b_scr):
    @pl.when(pl.program_id(1) == 0)
    def _():
        a_scr[...] = (oa_ref[...].astype(F32) * _silu(za_ref[...].astype(F32))).astype(a_scr.dtype)
        b_scr[...] = (ob_ref[...].astype(F32) * _silu(zb_ref[...].astype(F32))).astype(b_scr.dtype)

    ya = jnp.dot(a_scr[...], wa_ref[...], preferred_element_type=F32)
    yb = jnp.dot(b_scr[...], wb_ref[...], preferred_element_type=F32)
    u = (jax.nn.sigmoid(ga_ref[...].astype(F32)) * ya + jax.nn.sigmoid(gb_ref[...].astype(F32)) * yb)
    u_ref[...] = u.astype(u_ref.dtype)


def _merge1(o_a, o_b, px, wpa, wpb, *, d, tm, tn):
    t = o_a.shape[0]
    nj = d // tn
    return pl.pallas_call(
        _merge1_kernel,
        out_shape=jax.ShapeDtypeStruct((t, d), BF16),
        grid=(t // tm, nj),
        in_specs=[pl.BlockSpec((tm, d), lambda i, j: (i, 0)),
                  pl.BlockSpec((tm, d), lambda i, j: (i, 1)),
                  pl.BlockSpec((tm, d), lambda i, j: (i, 0)),
                  pl.BlockSpec((tm, d), lambda i, j: (i, 3)),
                  pl.BlockSpec((tm, tn), lambda i, j: (i, 4 * nj + j)),
                  pl.BlockSpec((tm, tn), lambda i, j: (i, 5 * nj + j)),
                  pl.BlockSpec((d, tn), lambda i, j: (0, j)),
                  pl.BlockSpec((d, tn), lambda i, j: (0, j))],
        out_specs=pl.BlockSpec((tm, tn), lambda i, j: (i, j)),
        scratch_shapes=[pltpu.VMEM((tm, d), BF16), pltpu.VMEM((tm, d), BF16)],
        compiler_params=_cparams("parallel", "arbitrary"),
        name="merge_gate",
    )(o_a, px, o_b, px, px, px, wpa, wpb)


def _merge2_kernel(u_ref, wo_ref, x_ref, g_ref, mod_ref, o_ref, *, d):
    y = jnp.dot(u_ref[...], wo_ref[...], preferred_element_type=F32)
    n = y * lax.rsqrt(jnp.mean(y * y, axis=-1, keepdims=True) + EPS) * g_ref[...]
    o_ref[...] = x_ref[...] + mod_ref[:, 2 * d:3 * d] * n


def _merge2(u, wo, x2, g_post, mod3, row_of_tile, tm):
    t, d = x2.shape
    return pl.pallas_call(
        functools.partial(_merge2_kernel, d=d),
        out_shape=jax.ShapeDtypeStruct((t, d), F32),
        grid=(t // tm,),
        in_specs=[pl.BlockSpec((tm, d), lambda i: (i, 0)),
                  pl.BlockSpec((d, d), lambda i: (0, 0)),
                  pl.BlockSpec((tm, d), lambda i: (i, 0)),
                  pl.BlockSpec((1, d), lambda i: (0, 0)),
                  pl.BlockSpec((None, 1, 3 * d), lambda i: (row_of_tile(i), 0, 0))],
        out_specs=pl.BlockSpec((tm, d), lambda i: (i, 0)),
        compiler_params=_cparams("parallel"),
        name="merge_out",
    )(u, wo, x2, g_post.reshape(1, d), mod3)


def _rope_tables(seq):
    rows = seq // GRID_W
    r = jnp.repeat(jnp.arange(rows, dtype=F32), GRID_W)
    col = jnp.tile(jnp.arange(GRID_W, dtype=F32), rows)
    n_freq = HEAD_DIM // 4
    inv = ROPE_THETA ** (-jnp.arange(n_freq, dtype=F32) / n_freq)
    ang = jnp.concatenate([r[:, None] * inv, col[:, None] * inv], axis=-1)
    cos, sin = jnp.cos(ang), jnp.sin(ang)
    return jnp.concatenate([cos, cos], axis=-1), jnp.concatenate([-sin, sin], axis=-1)


def _regroup_w_in(w, d):
    kv = d // A_GROUP
    k_a, v_a = w[:, 0:kv], w[:, kv:2 * kv]
    k_b, v_b = w[:, 2 * kv:2 * kv + d], w[:, 2 * kv + d:2 * kv + 2 * d]
    q = w[:, 2 * kv + 2 * d:]
    return jnp.concatenate([q, k_b, v_b, k_a, v_a], axis=-1).astype(BF16)


def _tile_types(d, tn, rope):
    per = d // tn
    kv = d // A_GROUP // tn
    r = 1 if rope else 0
    seg = [(per, r | 2), (per, 0), (per, r | 2), (per, 0), (per, 0), (per, 0),
           (per, r), (per, 0), (kv, r), (kv, 0)]
    out = []
    for n, ty in seg:
        out += [ty] * n
    return jnp.asarray(out, jnp.int32)


def kernel(x, c, ctx, c_ctx, w_ada, b_ada, g_pre, g_post, w_in, sink, lam_qk, g_subln,
           w_proj_a, w_proj_b, w_out):
    batch, seq, d = x.shape
    ctx_len = ctx.shape[1]
    depth = w_in.shape[0]
    t_lat, t_ctx = batch * seq, batch * ctx_len
    kv_cols = 2 * d + 2 * (d // A_GROUP)
    q_cols = 6 * d
    in_cols = q_cols + kv_cols

    tn_in = min(512, d // A_GROUP)
    tm_lat = _pick(seq, (1024, 512, 256))
    tm_ctx = _pick(t_ctx, (512, 256))
    tq_a = _pick(seq, (512, 256, 128))
    tq_b = _pick(seq, (1024, 512, 256))
    tk_b = _pick(seq, (1024, 512, 256))
    tn_m = _pick(d, (1024, 512))
    tm_m = _pick(seq, (512, 256))

    cosf, sinf = _rope_tables(seq)
    zeros_tab = jnp.zeros((tm_ctx, HEAD_DIM), F32)
    types_lat = _tile_types(d, tn_in, rope=True)
    types_ctx = _tile_types(d, tn_in, rope=False)
    cvec8 = jnp.concatenate([c, c_ctx[None, :], jnp.zeros((8 - batch - 1, d), F32)], axis=0)

    x2 = x.reshape(t_lat, d)
    c2 = ctx.reshape(t_ctx, d)
    lat_row = lambda tm: (lambda i: i // (seq // tm))
    ctx_row = lambda i: batch

    for l in range(depth):
        last = l == depth - 1
        lam_init = 0.8 - 0.6 * math.exp(-0.3 * l)
        w_l = _regroup_w_in(w_in[l], d)
        wpa, wpb, wo = w_proj_a[l].astype(BF16), w_proj_b[l].astype(BF16), w_out[l].astype(BF16)
        mod3 = _ada(cvec8, w_ada[l], b_ada[l]).reshape(8, 1, 3 * d)

        hx = _prenorm(x2, g_pre[l], mod3, lat_row(tm_lat), tm_lat)
        hc = _prenorm(c2, g_pre[l], mod3, ctx_row, tm_ctx)
        pxl = _inproj(hx, w_l, types_lat, cosf, sinf, col0=0, ncols=in_cols,
                      tm=tm_lat, tn=tn_in, pos_tiles=seq // tm_lat)
        if last:
            pxc = _inproj(hc, w_l, types_ctx, zeros_tab, zeros_tab, col0=q_cols, ncols=kv_cols,
                          tm=tm_ctx, tn=tn_in, pos_tiles=1)
            kv0_c = 0
        else:
            pxc = _inproj(hc, w_l, types_ctx, zeros_tab, zeros_tab, col0=0, ncols=in_cols,
                          tm=tm_ctx, tn=tn_in, pos_tiles=1)
            kv0_c = q_cols

        o_a = _attn_a(pxl, pxc, sink[l], batch=batch, seq=seq, ctx_len=ctx_len, d=d,
                      kv0_l=q_cols, kv0_c=kv0_c, tq=tq_a)
        o_b = _attn_b(pxl, pxl, pxc, lam_qk[l], g_subln[l], lam_init=lam_init, batch=batch,
                      rows_q=seq, rows_k=seq, ctx_len=ctx_len, d=d, kv0_k=q_cols, kv0_c=kv0_c,
                      tq=tq_b, tk=tk_b, name="attn_b_latent")
        u = _merge1(o_a, o_b, pxl, wpa, wpb, d=d, tm=tm_m, tn=tn_m)

        if not last:
            oc_a = _attn_a_ctx(pxc, sink[l], batch=batch, ctx_len=ctx_len, d=d, kv0_c=kv0_c)
            oc_b = _attn_b(pxc, pxc, None, lam_qk[l], g_subln[l], lam_init=lam_init, batch=batch,
                           rows_q=ctx_len, rows_k=ctx_len, ctx_len=ctx_len, d=d, kv0_k=kv0_c,
                           kv0_c=kv0_c, tq=ctx_len, tk=ctx_len, name="attn_b_context")
            uc = _merge1(oc_a, oc_b, pxc, wpa, wpb, d=d, tm=tm_ctx, tn=tn_m)
            c2 = _merge2(uc, wo, c2, g_post[l], mod3, ctx_row, tm_ctx)

        x2 = _merge2(u, wo, x2, g_post[l], mod3, lat_row(tm_m), tm_m)

    return x2.reshape(batch, seq, d)
```

```python
import functools
import math

import jax
import jax.numpy as jnp
from jax import lax
from jax.experimental import pallas as pl
from jax.experimental.pallas import tpu as pltpu

HEAD_DIM = 128
GRID_W = 64
WINDOW = 128
A_GROUP = 4
ROPE_THETA = 10000.0
EPS = 1e-6
NEG = -1e30
Q_SCALE = HEAD_DIM ** -0.5
LOG2E = math.log2(math.e)
VMEM_LIMIT_BYTES = 56 * 1024 * 1024

F32 = jnp.float32
BF16 = jnp.bfloat16


def _cparams(*semantics):
    return pltpu.CompilerParams(dimension_semantics=semantics, vmem_limit_bytes=VMEM_LIMIT_BYTES)


def _pick(n, candidates):
    for c in candidates:
        if n % c == 0:
            return c
    raise ValueError(f"no tile in {candidates} divides {n}")


def _silu(v):
    return v * jax.nn.sigmoid(v)


def _ada_kernel(c_ref, w_ref, b_ref, o_ref):
    a = _silu(c_ref[...])
    o_ref[...] = jnp.dot(a, w_ref[...], preferred_element_type=F32,
                         precision=lax.Precision.HIGHEST) + b_ref[...]


def _ada(cvec8, w, b):
    d, n = w.shape
    tn = _pick(n, (1024, 768, 512, 384, 256, 128))
    return pl.pallas_call(
        _ada_kernel,
        out_shape=jax.ShapeDtypeStruct((8, n), F32),
        grid=(n // tn,),
        in_specs=[pl.BlockSpec((8, d), lambda j: (0, 0)),
                  pl.BlockSpec((d, tn), lambda j: (0, j)),
                  pl.BlockSpec((1, tn), lambda j: (0, j))],
        out_specs=pl.BlockSpec((8, tn), lambda j: (0, j)),
        compiler_params=_cparams("arbitrary"),
        name="ada_ln",
    )(cvec8, w, b.reshape(1, n))


def _prenorm_kernel(x_ref, g_ref, mod_ref, o_ref, *, d):
    x = x_ref[...]
    y = x * lax.rsqrt(jnp.mean(x * x, axis=-1, keepdims=True) + EPS)
    y = y * g_ref[...]
    shift = mod_ref[:, 0:d]
    scale = mod_ref[:, d:2 * d]
    o_ref[...] = (y * (1.0 + scale) + shift).astype(o_ref.dtype)


def _prenorm(x2, g, mod3, row_of_tile, tm):
    t, d = x2.shape
    return pl.pallas_call(
        functools.partial(_prenorm_kernel, d=d),
        out_shape=jax.ShapeDtypeStruct((t, d), BF16),
        grid=(t // tm,),
        in_specs=[pl.BlockSpec((tm, d), lambda i: (i, 0)),
                  pl.BlockSpec((1, d), lambda i: (0, 0)),
                  pl.BlockSpec((None, 1, 3 * d), lambda i: (row_of_tile(i), 0, 0))],
        out_specs=pl.BlockSpec((tm, d), lambda i: (i, 0)),
        compiler_params=_cparams("parallel"),
        name="prenorm",
    )(x2, g.reshape(1, d), mod3)


def _inproj_kernel(type_ref, h_ref, w_ref, cos_ref, sin_ref, o_ref, *, tile0):
    t = type_ref[tile0 + pl.program_id(1)]
    heads = o_ref.shape[1] // HEAD_DIM

    def matmul():
        return jnp.dot(h_ref[...], w_ref[...], preferred_element_type=F32)

    @pl.when(t == 0)
    def _():
        o_ref[...] = matmul().astype(o_ref.dtype)

    @pl.when(t != 0)
    def _():
        acc = matmul()
        rope = (t & 1).astype(F32)
        scale = (jnp.where((t & 2) != 0, Q_SCALE, 1.0) * jnp.where((t & 4) != 0, LOG2E, 1.0)).astype(F32)
        ca = (rope * cos_ref[...] + (1.0 - rope)) * scale
        sa = (rope * sin_ref[...]) * scale
        for hh in range(heads):
            a = acc[:, hh * HEAD_DIM:(hh + 1) * HEAD_DIM]
            r = a * ca + pltpu.roll(a, HEAD_DIM // 2, 1) * sa
            o_ref[:, hh * HEAD_DIM:(hh + 1) * HEAD_DIM] = r.astype(o_ref.dtype)


def _inproj(h, w, types, cosf, sinf, *, col0, ncols, tm, tn, pos_tiles):
    t, d = h.shape
    tile0 = col0 // tn
    grid_spec = pltpu.PrefetchScalarGridSpec(
        num_scalar_prefetch=1,
        grid=(t // tm, ncols // tn),
        in_specs=[pl.BlockSpec((tm, d), lambda i, j, ty: (i, 0)),
                  pl.BlockSpec((d, tn), lambda i, j, ty: (0, tile0 + j)),
                  pl.BlockSpec((tm, HEAD_DIM), lambda i, j, ty: (i % pos_tiles, 0)),
                  pl.BlockSpec((tm, HEAD_DIM), lambda i, j, ty: (i % pos_tiles, 0))],
        out_specs=pl.BlockSpec((tm, tn), lambda i, j, ty: (i, j)),
    )
    return pl.pallas_call(
        functools.partial(_inproj_kernel, tile0=tile0),
        out_shape=jax.ShapeDtypeStruct((t, ncols), BF16),
        grid_spec=grid_spec,
        compiler_params=_cparams("parallel", "arbitrary"),
        name="in_proj",
    )(types, h, w, cosf, sinf)


def _stack_heads(q):
    g = q.shape[1] // HEAD_DIM
    return jnp.concatenate([q[:, i * HEAD_DIM:(i + 1) * HEAD_DIM] for i in range(g)], axis=0)


def _unstack_heads(o, g):
    rows = o.shape[0] // g
    return jnp.concatenate([o[i * rows:(i + 1) * rows, :] for i in range(g)], axis=1)


def _sink_column(sink_ref, kvh, rows):
    return jnp.concatenate(
        [jnp.full((rows, 1), sink_ref[kvh * A_GROUP + g], F32) for g in range(A_GROUP)], axis=0)


def _qkt(q, k):
    return lax.dot_general(q, k, (((1,), (1,)), ((), ())), preferred_element_type=F32)


def _attn_a_kernel(sink_ref, q_ref, kp_ref, km_ref, kn_ref, vp_ref, vm_ref, vn_ref, kc_ref, vc_ref,
                   o_ref, *, seq):
    i = pl.program_id(1)
    kvh = pl.program_id(2)
    tq = q_ref.shape[0]
    k_win = jnp.concatenate([kp_ref[...], km_ref[...], kn_ref[...]], axis=0)
    v_win = jnp.concatenate([vp_ref[...], vm_ref[...], vn_ref[...]], axis=0)
    kc = kc_ref[...]
    vc = vc_ref[...]
    rows = A_GROUP * WINDOW
    sink = _sink_column(sink_ref, kvh, WINDOW)
    qi = lax.broadcasted_iota(jnp.int32, (rows, 3 * WINDOW), 0) & (WINDOW - 1)
    kj = lax.broadcasted_iota(jnp.int32, (rows, 3 * WINDOW), 1)
    band = jnp.abs(qi + WINDOW - kj) <= WINDOW
    for r in range(tq // WINDOW):
        q = _stack_heads(q_ref[r * WINDOW:(r + 1) * WINDOW, :])
        k = k_win[r * WINDOW:(r + 3) * WINDOW, :]
        v = v_win[r * WINDOW:(r + 3) * WINDOW, :]
        base = i * tq + (r - 1) * WINDOW
        valid = band & (kj >= -base) & (kj < seq - base)
        s_loc = jnp.where(valid, _qkt(q, k), NEG)
        s_ctx = _qkt(q, kc)
        m = jnp.maximum(jnp.maximum(jnp.max(s_loc, axis=-1, keepdims=True),
                                    jnp.max(s_ctx, axis=-1, keepdims=True)), sink)
        p_loc = jnp.exp(s_loc - m)
        p_ctx = jnp.exp(s_ctx - m)
        denom = (jnp.sum(p_loc, axis=-1, keepdims=True) + jnp.sum(p_ctx, axis=-1, keepdims=True)
                 + jnp.exp(sink - m))
        o = (jnp.dot(p_loc.astype(v.dtype), v, preferred_element_type=F32)
             + jnp.dot(p_ctx.astype(vc.dtype), vc, preferred_element_type=F32)) / denom
        o_ref[r * WINDOW:(r + 1) * WINDOW, :] = _unstack_heads(o, A_GROUP).astype(o_ref.dtype)


def _attn_a(pxl, pxc, sink, *, batch, seq, ctx_len, d, kv0_l, kv0_c, tq):
    kvh = d // HEAD_DIM // A_GROUP
    gw = A_GROUP * HEAD_DIM
    nq = seq // tq
    r = tq // WINDOW
    nblk = seq // WINDOW
    ka_l = (kv0_l + 2 * d) // HEAD_DIM
    va_l = ka_l + kvh
    ka_c = (kv0_c + 2 * d) // HEAD_DIM
    va_c = ka_c + kvh

    def prev(b, i, h):
        return (b * nblk + jnp.maximum(i * r - 1, 0), h)

    def nxt(b, i, h):
        return (b * nblk + jnp.minimum((i + 1) * r, nblk - 1), h)

    halo = (WINDOW, HEAD_DIM)
    return pl.pallas_call(
        functools.partial(_attn_a_kernel, seq=seq),
        out_shape=jax.ShapeDtypeStruct((batch * seq, d), BF16),
        grid=(batch, nq, kvh),
        in_specs=[
            pl.BlockSpec(memory_space=pltpu.SMEM),
            pl.BlockSpec((tq, gw), lambda b, i, h: (b * nq + i, h)),
            pl.BlockSpec(halo, lambda b, i, h: prev(b, i, ka_l + h)),
            pl.BlockSpec((tq, HEAD_DIM), lambda b, i, h: (b * nq + i, ka_l + h)),
            pl.BlockSpec(halo, lambda b, i, h: nxt(b, i, ka_l + h)),
            pl.BlockSpec(halo, lambda b, i, h: prev(b, i, va_l + h)),
            pl.BlockSpec((tq, HEAD_DIM), lambda b, i, h: (b * nq + i, va_l + h)),
            pl.BlockSpec(halo, lambda b, i, h: nxt(b, i, va_l + h)),
            pl.BlockSpec((ctx_len, HEAD_DIM), lambda b, i, h: (b, ka_c + h)),
            pl.BlockSpec((ctx_len, HEAD_DIM), lambda b, i, h: (b, va_c + h)),
        ],
        out_specs=pl.BlockSpec((tq, gw), lambda b, i, h: (b * nq + i, h)),
        compiler_params=_cparams("parallel", "parallel", "arbitrary"),
        name="attn_a_latent",
    )(sink, pxl, pxl, pxl, pxl, pxl, pxl, pxl, pxc, pxc)


def _attn_a_ctx_kernel(sink_ref, q_ref, k_ref, v_ref, o_ref):
    kvh = pl.program_id(1)
    rows = q_ref.shape[0]
    q = _stack_heads(q_ref[...])
    v = v_ref[...]
    sink = _sink_column(sink_ref, kvh, rows)
    s = _qkt(q, k_ref[...])
    m = jnp.maximum(jnp.max(s, axis=-1, keepdims=True), sink)
    p = jnp.exp(s - m)
    denom = jnp.sum(p, axis=-1, keepdims=True) + jnp.exp(sink - m)
    o = jnp.dot(p.astype(v.dtype), v, preferred_element_type=F32) / denom
    o_ref[...] = _unstack_heads(o, A_GROUP).astype(o_ref.dtype)


def _attn_a_ctx(pxc, sink, *, batch, ctx_len, d, kv0_c):
    kvh = d // HEAD_DIM // A_GROUP
    gw = A_GROUP * HEAD_DIM
    ka_c = (kv0_c + 2 * d) // HEAD_DIM
    va_c = ka_c + kvh
    return pl.pallas_call(
        _attn_a_ctx_kernel,
        out_shape=jax.ShapeDtypeStruct((batch * ctx_len, d), BF16),
        grid=(batch, kvh),
        in_specs=[pl.BlockSpec(memory_space=pltpu.SMEM),
                  pl.BlockSpec((ctx_len, gw), lambda b, h: (b, h)),
                  pl.BlockSpec((ctx_len, HEAD_DIM), lambda b, h: (b, ka_c + h)),
                  pl.BlockSpec((ctx_len, HEAD_DIM), lambda b, h: (b, va_c + h))],
        out_specs=pl.BlockSpec((ctx_len, gw), lambda b, h: (b, h)),
        compiler_params=_cparams("parallel", "arbitrary"),
        name="attn_a_context",
    )(sink, pxc, pxc, pxc)


def _diff_finish(o0, l0, o1, l1, lamqk_ref, g_ref, lam_init):
    lq = lamqk_ref[...]
    lam = (jnp.exp(jnp.sum(lq[0:1] * lq[1:2], axis=-1, keepdims=True))
           - jnp.exp(jnp.sum(lq[2:3] * lq[3:4], axis=-1, keepdims=True)) + lam_init)
    o = o0 / l0 - lam * (o1 / l1)
    y = o * lax.rsqrt(jnp.mean(o * o, axis=-1, keepdims=True) + EPS) * g_ref[...]
    return y * (1.0 - lam_init)


def _attn_b_ctx_kernel(q_ref, k_ref, v_ref, lamqk_ref, g_ref, o_ref, *, lam_init):
    v = v_ref[...]
    parts = []
    for mp in range(2):
        cols = slice(mp * HEAD_DIM, (mp + 1) * HEAD_DIM)
        s = _qkt(q_ref[:, cols], k_ref[:, cols])
        p = jnp.exp2(s - jnp.max(s, axis=-1, keepdims=True))
        parts += [jnp.dot(p.astype(v.dtype), v, preferred_element_type=F32),
                  jnp.sum(p, axis=-1, keepdims=True)]
    o_ref[...] = _diff_finish(*parts, lamqk_ref, g_ref, lam_init).astype(o_ref.dtype)


def _attn_b_ctx(pxc, lamqk, g_sub, *, lam_init, batch, ctx_len, d, kv0_c):
    hw = 2 * HEAD_DIM
    heads = d // hw
    qb, kb, vb = (2 * d) // hw, kv0_c // hw, (kv0_c + d) // hw
    return pl.pallas_call(
        functools.partial(_attn_b_ctx_kernel, lam_init=lam_init),
        out_shape=jax.ShapeDtypeStruct((batch * ctx_len, d), BF16),
        grid=(batch, heads),
        in_specs=[pl.BlockSpec((ctx_len, hw), lambda b, h: (b, qb + h)),
                  pl.BlockSpec((ctx_len, hw), lambda b, h: (b, kb + h)),
                  pl.BlockSpec((ctx_len, hw), lambda b, h: (b, vb + h)),
                  pl.BlockSpec((4, HEAD_DIM), lambda b, h: (0, 0)),
                  pl.BlockSpec((1, hw), lambda b, h: (0, 0))],
        out_specs=pl.BlockSpec((ctx_len, hw), lambda b, h: (b, h)),
        compiler_params=_cparams("parallel", "arbitrary"),
        name="attn_b_context",
    )(pxc, pxc, pxc, lamqk, g_sub.reshape(1, hw))


def _attn_b_latent_kernel(q_ref, k_ref, v_ref, kc_ref, vc_ref, lamqk_ref, g_ref, o_ref,
                          sa_scr, sb_scr, m_scr, l_scr, acc_scr, *, lam_init, tk):
    nk = k_ref.shape[0] // tk

    def scores(c, s_scr):
        start = pl.multiple_of(c * tk, tk)
        for mp in range(2):
            cols = slice(mp * HEAD_DIM, (mp + 1) * HEAD_DIM)
            s_scr[mp] = _qkt(q_ref[:, cols], k_ref[pl.ds(start, tk), cols])

    def absorb(s_of_map, v):
        for mp in range(2):
            s = s_of_map(mp)
            m_old = m_scr[mp]
            m_new = jnp.maximum(m_old, jnp.max(s, axis=-1, keepdims=True))
            alpha = jnp.exp2(m_old - m_new)
            p = jnp.exp2(s - m_new)
            l_scr[mp] = alpha * l_scr[mp] + jnp.sum(p, axis=-1, keepdims=True)
            acc_scr[mp] = alpha * acc_scr[mp] + jnp.dot(p.astype(v.dtype), v,
                                                        preferred_element_type=F32)
            m_scr[mp] = m_new

    def values(c):
        return v_ref[pl.ds(pl.multiple_of(c * tk, tk), tk), :]

    m_scr[...] = jnp.full(m_scr.shape, -jnp.inf, F32)
    l_scr[...] = jnp.zeros(l_scr.shape, F32)
    acc_scr[...] = jnp.zeros(acc_scr.shape, F32)

    scores(0, sa_scr)
    absorb(lambda mp: _qkt(q_ref[:, mp * HEAD_DIM:(mp + 1) * HEAD_DIM],
                           kc_ref[:, mp * HEAD_DIM:(mp + 1) * HEAD_DIM]), vc_ref[...])

    def pair(i, carry):
        scores(2 * i + 1, sb_scr)
        absorb(lambda mp: sa_scr[mp], values(2 * i))
        scores(2 * i + 2, sa_scr)
        absorb(lambda mp: sb_scr[mp], values(2 * i + 1))
        return carry

    lax.fori_loop(0, nk // 2 - 1, pair, 0)
    scores(nk - 1, sb_scr)
    absorb(lambda mp: sa_scr[mp], values(nk - 2))
    absorb(lambda mp: sb_scr[mp], values(nk - 1))

    o_ref[...] = _diff_finish(acc_scr[0], l_scr[0], acc_scr[1], l_scr[1], lamqk_ref, g_ref,
                              lam_init).astype(o_ref.dtype)


def _attn_b_latent(pxl, pxc, lamqk, g_sub, *, lam_init, batch, seq, ctx_len, d, kv0_l, kv0_c, tq, tk):
    hw = 2 * HEAD_DIM
    heads = d // hw
    nq = seq // tq
    assert (seq // tk) % 2 == 0 and seq // tk >= 2
    qb = (2 * d) // hw
    kb_l, vb_l = kv0_l // hw, (kv0_l + d) // hw
    kb_c, vb_c = kv0_c // hw, (kv0_c + d) // hw
    return pl.pallas_call(
        functools.partial(_attn_b_latent_kernel, lam_init=lam_init, tk=tk),
        out_shape=jax.ShapeDtypeStruct((batch * seq, d), BF16),
        grid=(batch, heads, nq),
        in_specs=[pl.BlockSpec((tq, hw), lambda b, h, i: (b * nq + i, qb + h)),
                  pl.BlockSpec((seq, hw), lambda b, h, i: (b, kb_l + h)),
                  pl.BlockSpec((seq, hw), lambda b, h, i: (b, vb_l + h)),
                  pl.BlockSpec((ctx_len, hw), lambda b, h, i: (b, kb_c + h)),
                  pl.BlockSpec((ctx_len, hw), lambda b, h, i: (b, vb_c + h)),
                  pl.BlockSpec((4, HEAD_DIM), lambda b, h, i: (0, 0)),
                  pl.BlockSpec((1, hw), lambda b, h, i: (0, 0))],
        out_specs=pl.BlockSpec((tq, hw), lambda b, h, i: (b * nq + i, h)),
        scratch_shapes=[pltpu.VMEM((2, tq, tk), F32), pltpu.VMEM((2, tq, tk), F32),
                        pltpu.VMEM((2, tq, 1), F32), pltpu.VMEM((2, tq, 1), F32),
                        pltpu.VMEM((2, tq, hw), F32)],
        compiler_params=_cparams("parallel", "parallel", "arbitrary"),
        name="attn_b_latent",
    )(pxl, pxl, pxl, pxc, pxc, lamqk, g_sub.reshape(1, hw))


def _merge1_kernel(oa_ref, za_ref, ob_ref, zb_ref, ga_ref, gb_ref, wa_ref, wb_ref, u_ref,
                   a_scr, b_scr):
    @pl.when(pl.program_id(1) == 0)
    def _():
        a_scr[...] = (oa_ref[...].astype(F32) * _silu(za_ref[...].astype(F32))).astype(a_scr.dtype)
        b_scr[...] = (ob_ref[...].astype(F32) * _silu(zb_ref[...].astype(F32))).astype(b_scr.dtype)

    ya = jnp.dot(a_scr[...], wa_ref[...], preferred_element_type=F32)
    yb = jnp.dot(b_scr[...], wb_ref[...], preferred_element_type=F32)
    u = (jax.nn.sigmoid(ga_ref[...].astype(F32)) * ya + jax.nn.sigmoid(gb_ref[...].astype(F32)) * yb)
    u_ref[...] = u.astype(u_ref.dtype)


def _merge1(o_a, o_b, px, wpa, wpb, *, d, tm, tn):
    t = o_a.shape[0]
    nj = d // tn
    return pl.pallas_call(
        _merge1_kernel,
        out_shape=jax.ShapeDtypeStruct((t, d), BF16),
        grid=(t // tm, nj),
        in_specs=[pl.BlockSpec((tm, d), lambda i, j: (i, 0)),
                  pl.BlockSpec((tm, d), lambda i, j: (i, 1)),
                  pl.BlockSpec((tm, d), lambda i, j: (i, 0)),
                  pl.BlockSpec((tm, d), lambda i, j: (i, 3)),
                  pl.BlockSpec((tm, tn), lambda i, j: (i, 4 * nj + j)),
                  pl.BlockSpec((tm, tn), lambda i, j: (i, 5 * nj + j)),
                  pl.BlockSpec((d, tn), lambda i, j: (0, j)),
                  pl.BlockSpec((d, tn), lambda i, j: (0, j))],
        out_specs=pl.BlockSpec((tm, tn), lambda i, j: (i, j)),
        scratch_shapes=[pltpu.VMEM((tm, d), BF16), pltpu.VMEM((tm, d), BF16)],
        compiler_params=_cparams("parallel", "arbitrary"),
        name="merge_gate",
    )(o_a, px, o_b, px, px, px, wpa, wpb)


def _merge2_kernel(u_ref, wo_ref, x_ref, g_ref, mod_ref, o_ref, *, d):
    y = jnp.dot(u_ref[...], wo_ref[...], preferred_element_type=F32)
    n = y * lax.rsqrt(jnp.mean(y * y, axis=-1, keepdims=True) + EPS) * g_ref[...]
    o_ref[...] = x_ref[...] + mod_ref[:, 2 * d:3 * d] * n


def _merge2(u, wo, x2, g_post, mod3, row_of_tile, tm):
    t, d = x2.shape
    return pl.pallas_call(
        functools.partial(_merge2_kernel, d=d),
        out_shape=jax.ShapeDtypeStruct((t, d), F32),
        grid=(t // tm,),
        in_specs=[pl.BlockSpec((tm, d), lambda i: (i, 0)),
                  pl.BlockSpec((d, d), lambda i: (0, 0)),
                  pl.BlockSpec((tm, d), lambda i: (i, 0)),
                  pl.BlockSpec((1, d), lambda i: (0, 0)),
                  pl.BlockSpec((None, 1, 3 * d), lambda i: (row_of_tile(i), 0, 0))],
        out_specs=pl.BlockSpec((tm, d), lambda i: (i, 0)),
        compiler_params=_cparams("parallel"),
        name="merge_out",
    )(u, wo, x2, g_post.reshape(1, d), mod3)


def _rope_tables(seq):
    rows = seq // GRID_W
    r = jnp.repeat(jnp.arange(rows, dtype=F32), GRID_W)
    col = jnp.tile(jnp.arange(GRID_W, dtype=F32), rows)
    n_freq = HEAD_DIM // 4
    inv = ROPE_THETA ** (-jnp.arange(n_freq, dtype=F32) / n_freq)
    ang = jnp.concatenate([r[:, None] * inv, col[:, None] * inv], axis=-1)
    cos, sin = jnp.cos(ang), jnp.sin(ang)
    return jnp.concatenate([cos, cos], axis=-1), jnp.concatenate([-sin, sin], axis=-1)


def _regroup_w_in(w, d):
    kv = d // A_GROUP
    k_a, v_a = w[:, 0:kv], w[:, kv:2 * kv]
    k_b, v_b = w[:, 2 * kv:2 * kv + d], w[:, 2 * kv + d:2 * kv + 2 * d]
    q = w[:, 2 * kv + 2 * d:]
    return jnp.concatenate([q, k_b, v_b, k_a, v_a], axis=-1).astype(BF16)


def _tile_types(d, tn, rope):
    per = d // tn
    kv = d // A_GROUP // tn
    r = 1 if rope else 0
    seg = [(per, r | 2), (per, 0), (per, r | 2 | 4), (per, 0), (per, 0), (per, 0),
           (per, r), (per, 0), (kv, r), (kv, 0)]
    out = []
    for n, ty in seg:
        out += [ty] * n
    return jnp.asarray(out, jnp.int32)


def kernel(x, c, ctx, c_ctx, w_ada, b_ada, g_pre, g_post, w_in, sink, lam_qk, g_subln,
           w_proj_a, w_proj_b, w_out):
    batch, seq, d = x.shape
    ctx_len = ctx.shape[1]
    depth = w_in.shape[0]
    t_lat, t_ctx = batch * seq, batch * ctx_len
    kv_cols = 2 * d + 2 * (d // A_GROUP)
    q_cols = 6 * d
    in_cols = q_cols + kv_cols

    tn_in = min(512, d // A_GROUP)
    tm_lat = _pick(seq, (1024, 512, 256))
    tm_ctx = _pick(t_ctx, (512, 256))
    tq_a = _pick(seq, (512, 256, 128))
    tq_b = _pick(seq, (1024, 512, 256))
    tk_b = _pick(seq // 2, (512, 256))
    tn_m = _pick(d, (1024, 512))
    tm_m = _pick(seq, (512, 256))

    cosf, sinf = _rope_tables(seq)
    zeros_tab = jnp.zeros((tm_ctx, HEAD_DIM), F32)
    types_lat = _tile_types(d, tn_in, rope=True)
    types_ctx = _tile_types(d, tn_in, rope=False)
    cvec8 = jnp.concatenate([c, c_ctx[None, :], jnp.zeros((8 - batch - 1, d), F32)], axis=0)

    x2 = x.reshape(t_lat, d)
    c2 = ctx.reshape(t_ctx, d)
    lat_row = lambda tm: (lambda i: i // (seq // tm))
    ctx_row = lambda i: batch

    for l in range(depth):
        last = l == depth - 1
        lam_init = 0.8 - 0.6 * math.exp(-0.3 * l)
        w_l = _regroup_w_in(w_in[l], d)
        wpa, wpb, wo = w_proj_a[l].astype(BF16), w_proj_b[l].astype(BF16), w_out[l].astype(BF16)
        mod3 = _ada(cvec8, w_ada[l], b_ada[l]).reshape(8, 1, 3 * d)

        hx = _prenorm(x2, g_pre[l], mod3, lat_row(tm_lat), tm_lat)
        hc = _prenorm(c2, g_pre[l], mod3, ctx_row, tm_ctx)
        pxl = _inproj(hx, w_l, types_lat, cosf, sinf, col0=0, ncols=in_cols,
                      tm=tm_lat, tn=tn_in, pos_tiles=seq // tm_lat)
        if last:
            pxc = _inproj(hc, w_l, types_ctx, zeros_tab, zeros_tab, col0=q_cols, ncols=kv_cols,
                          tm=tm_ctx, tn=tn_in, pos_tiles=1)
            kv0_c = 0
        else:
            pxc = _inproj(hc, w_l, types_ctx, zeros_tab, zeros_tab, col0=0, ncols=in_cols,
                          tm=tm_ctx, tn=tn_in, pos_tiles=1)
            kv0_c = q_cols

        o_a = _attn_a(pxl, pxc, sink[l], batch=batch, seq=seq, ctx_len=ctx_len, d=d,
                      kv0_l=q_cols, kv0_c=kv0_c, tq=tq_a)
        o_b = _attn_b_latent(pxl, pxc, lam_qk[l], g_subln[l], lam_init=lam_init, batch=batch,
                             seq=seq, ctx_len=ctx_len, d=d, kv0_l=q_cols, kv0_c=kv0_c,
                             tq=tq_b, tk=tk_b)
        u = _merge1(o_a, o_b, pxl, wpa, wpb, d=d, tm=tm_m, tn=tn_m)

        if not last:
            oc_a = _attn_a_ctx(pxc, sink[l], batch=batch, ctx_len=ctx_len, d=d, kv0_c=kv0_c)
            oc_b = _attn_b_ctx(pxc, lam_qk[l], g_subln[l], lam_init=lam_init, batch=batch,
                               ctx_len=ctx_len, d=d, kv0_c=kv0_c)
            uc = _merge1(oc_a, oc_b, pxc, wpa, wpb, d=d, tm=tm_ctx, tn=tn_m)
            c2 = _merge2(uc, wo, c2, g_post[l], mod3, ctx_row, tm_ctx)

        x2 = _merge2(u, wo, x2, g_post[l], mod3, lat_row(tm_m), tm_m)

    return x2.reshape(batch, seq, d)
```

```python
import functools
import math

import jax
import jax.numpy as jnp
from jax import lax
from jax.experimental import pallas as pl
from jax.experimental.pallas import tpu as pltpu

HEAD_DIM = 128
GRID_W = 64
WINDOW = 128
A_GROUP = 4
ROPE_THETA = 10000.0
EPS = 1e-6
NEG = -1e30
Q_SCALE = HEAD_DIM ** -0.5
LOG2E = math.log2(math.e)
VMEM_LIMIT_BYTES = 56 * 1024 * 1024

F32 = jnp.float32
BF16 = jnp.bfloat16


def _cparams(*semantics):
    return pltpu.CompilerParams(dimension_semantics=semantics, vmem_limit_bytes=VMEM_LIMIT_BYTES)


def _pick(n, candidates):
    for c in candidates:
        if n % c == 0:
            return c
    raise ValueError(f"no tile in {candidates} divides {n}")


def _silu(v):
    return v * jax.nn.sigmoid(v)


def _ada_kernel(c_ref, w_ref, b_ref, o_ref):
    a = _silu(c_ref[...])
    o_ref[...] = jnp.dot(a, w_ref[...], preferred_element_type=F32,
                         precision=lax.Precision.HIGHEST) + b_ref[...]


def _ada(cvec8, w, b):
    d, n = w.shape
    tn = _pick(n, (1024, 768, 512, 384, 256, 128))
    return pl.pallas_call(
        _ada_kernel,
        out_shape=jax.ShapeDtypeStruct((8, n), F32),
        grid=(n // tn,),
        in_specs=[pl.BlockSpec((8, d), lambda j: (0, 0)),
                  pl.BlockSpec((d, tn), lambda j: (0, j)),
                  pl.BlockSpec((1, tn), lambda j: (0, j))],
        out_specs=pl.BlockSpec((8, tn), lambda j: (0, j)),
        compiler_params=_cparams("arbitrary"),
        name="ada_ln",
    )(cvec8, w, b.reshape(1, n))


def _prenorm_kernel(x_ref, g_ref, mod_ref, o_ref, *, d):
    x = x_ref[...]
    y = x * lax.rsqrt(jnp.mean(x * x, axis=-1, keepdims=True) + EPS)
    y = y * g_ref[...]
    shift = mod_ref[:, 0:d]
    scale = mod_ref[:, d:2 * d]
    o_ref[...] = (y * (1.0 + scale) + shift).astype(o_ref.dtype)


def _prenorm(x2, g, mod3, row_of_tile, tm):
    t, d = x2.shape
    return pl.pallas_call(
        functools.partial(_prenorm_kernel, d=d),
        out_shape=jax.ShapeDtypeStruct((t, d), BF16),
        grid=(t // tm,),
        in_specs=[pl.BlockSpec((tm, d), lambda i: (i, 0)),
                  pl.BlockSpec((1, d), lambda i: (0, 0)),
                  pl.BlockSpec((None, 1, 3 * d), lambda i: (row_of_tile(i), 0, 0))],
        out_specs=pl.BlockSpec((tm, d), lambda i: (i, 0)),
        compiler_params=_cparams("parallel"),
        name="prenorm",
    )(x2, g.reshape(1, d), mod3)


def _inproj_kernel(type_ref, h_ref, w_ref, cos_ref, sin_ref, o_ref, *, tile0):
    t = type_ref[tile0 + pl.program_id(1)]
    heads = o_ref.shape[1] // HEAD_DIM

    def matmul():
        return jnp.dot(h_ref[...], w_ref[...], preferred_element_type=F32)

    @pl.when(t == 0)
    def _():
        o_ref[...] = matmul().astype(o_ref.dtype)

    @pl.when(t != 0)
    def _():
        acc = matmul()
        rope = (t & 1).astype(F32)
        scale = (jnp.where((t & 2) != 0, Q_SCALE, 1.0) * jnp.where((t & 4) != 0, LOG2E, 1.0)).astype(F32)
        ca = (rope * cos_ref[...] + (1.0 - rope)) * scale
        sa = (rope * sin_ref[...]) * scale
        for hh in range(heads):
            a = acc[:, hh * HEAD_DIM:(hh + 1) * HEAD_DIM]
            r = a * ca + pltpu.roll(a, HEAD_DIM // 2, 1) * sa
            o_ref[:, hh * HEAD_DIM:(hh + 1) * HEAD_DIM] = r.astype(o_ref.dtype)


def _inproj(h, w, types, cosf, sinf, *, col0, ncols, tm, tn, pos_tiles):
    t, d = h.shape
    tile0 = col0 // tn
    grid_spec = pltpu.PrefetchScalarGridSpec(
        num_scalar_prefetch=1,
        grid=(t // tm, ncols // tn),
        in_specs=[pl.BlockSpec((tm, d), lambda i, j, ty: (i, 0)),
                  pl.BlockSpec((d, tn), lambda i, j, ty: (0, tile0 + j)),
                  pl.BlockSpec((tm, HEAD_DIM), lambda i, j, ty: (i % pos_tiles, 0)),
                  pl.BlockSpec((tm, HEAD_DIM), lambda i, j, ty: (i % pos_tiles, 0))],
        out_specs=pl.BlockSpec((tm, tn), lambda i, j, ty: (i, j)),
    )
    return pl.pallas_call(
        functools.partial(_inproj_kernel, tile0=tile0),
        out_shape=jax.ShapeDtypeStruct((t, ncols), BF16),
        grid_spec=grid_spec,
        compiler_params=_cparams("parallel", "arbitrary"),
        name="in_proj",
    )(types, h, w, cosf, sinf)


def _stack_heads(q):
    g = q.shape[1] // HEAD_DIM
    return jnp.concatenate([q[:, i * HEAD_DIM:(i + 1) * HEAD_DIM] for i in range(g)], axis=0)


def _unstack_heads(o, g):
    rows = o.shape[0] // g
    return jnp.concatenate([o[i * rows:(i + 1) * rows, :] for i in range(g)], axis=1)


def _sink_column(sink_ref, kvh, rows):
    return jnp.concatenate(
        [jnp.full((rows, 1), sink_ref[kvh * A_GROUP + g], F32) for g in range(A_GROUP)], axis=0)


def _qkt(q, k):
    return lax.dot_general(q, k, (((1,), (1,)), ((), ())), preferred_element_type=F32)


def _attn_a_kernel(sink_ref, q_ref, kp_ref, km_ref, kn_ref, vp_ref, vm_ref, vn_ref, kc_ref, vc_ref,
                   o_ref, *, seq):
    i = pl.program_id(1)
    kvh = pl.program_id(2)
    tq = q_ref.shape[0]
    k_win = jnp.concatenate([kp_ref[...], km_ref[...], kn_ref[...]], axis=0)
    v_win = jnp.concatenate([vp_ref[...], vm_ref[...], vn_ref[...]], axis=0)
    kc = kc_ref[...]
    vc = vc_ref[...]
    rows = A_GROUP * WINDOW
    sink = _sink_column(sink_ref, kvh, WINDOW)
    qi = lax.broadcasted_iota(jnp.int32, (rows, 3 * WINDOW), 0) & (WINDOW - 1)
    kj = lax.broadcasted_iota(jnp.int32, (rows, 3 * WINDOW), 1)
    band = jnp.abs(qi + WINDOW - kj) <= WINDOW
    for r in range(tq // WINDOW):
        q = _stack_heads(q_ref[r * WINDOW:(r + 1) * WINDOW, :])
        k = k_win[r * WINDOW:(r + 3) * WINDOW, :]
        v = v_win[r * WINDOW:(r + 3) * WINDOW, :]
        base = i * tq + (r - 1) * WINDOW
        valid = band & (kj >= -base) & (kj < seq - base)
        s_loc = jnp.where(valid, _qkt(q, k), NEG)
        s_ctx = _qkt(q, kc)
        m = jnp.maximum(jnp.maximum(jnp.max(s_loc, axis=-1, keepdims=True),
                                    jnp.max(s_ctx, axis=-1, keepdims=True)), sink)
        p_loc = jnp.exp(s_loc - m)
        p_ctx = jnp.exp(s_ctx - m)
        denom = (jnp.sum(p_loc, axis=-1, keepdims=True) + jnp.sum(p_ctx, axis=-1, keepdims=True)
                 + jnp.exp(sink - m))
        o = (jnp.dot(p_loc.astype(v.dtype), v, preferred_element_type=F32)
             + jnp.dot(p_ctx.astype(vc.dtype), vc, preferred_element_type=F32)) / denom
        o_ref[r * WINDOW:(r + 1) * WINDOW, :] = _unstack_heads(o, A_GROUP).astype(o_ref.dtype)


def _attn_a(pxl, pxc, sink, *, batch, seq, ctx_len, d, kv0_l, kv0_c, tq):
    kvh = d // HEAD_DIM // A_GROUP
    gw = A_GROUP * HEAD_DIM
    nq = seq // tq
    r = tq // WINDOW
    nblk = seq // WINDOW
    ka_l = (kv0_l + 2 * d) // HEAD_DIM
    va_l = ka_l + kvh
    ka_c = (kv0_c + 2 * d) // HEAD_DIM
    va_c = ka_c + kvh

    def prev(b, i, h):
        return (b * nblk + jnp.maximum(i * r - 1, 0), h)

    def nxt(b, i, h):
        return (b * nblk + jnp.minimum((i + 1) * r, nblk - 1), h)

    halo = (WINDOW, HEAD_DIM)
    return pl.pallas_call(
        functools.partial(_attn_a_kernel, seq=seq),
        out_shape=jax.ShapeDtypeStruct((batch * seq, d), BF16),
        grid=(batch, nq, kvh),
        in_specs=[
            pl.BlockSpec(memory_space=pltpu.SMEM),
            pl.BlockSpec((tq, gw), lambda b, i, h: (b * nq + i, h)),
            pl.BlockSpec(halo, lambda b, i, h: prev(b, i, ka_l + h)),
            pl.BlockSpec((tq, HEAD_DIM), lambda b, i, h: (b * nq + i, ka_l + h)),
            pl.BlockSpec(halo, lambda b, i, h: nxt(b, i, ka_l + h)),
            pl.BlockSpec(halo, lambda b, i, h: prev(b, i, va_l + h)),
            pl.BlockSpec((tq, HEAD_DIM), lambda b, i, h: (b * nq + i, va_l + h)),
            pl.BlockSpec(halo, lambda b, i, h: nxt(b, i, va_l + h)),
            pl.BlockSpec((ctx_len, HEAD_DIM), lambda b, i, h: (b, ka_c + h)),
            pl.BlockSpec((ctx_len, HEAD_DIM), lambda b, i, h: (b, va_c + h)),
        ],
        out_specs=pl.BlockSpec((tq, gw), lambda b, i, h: (b * nq + i, h)),
        compiler_params=_cparams("parallel", "parallel", "arbitrary"),
        name="attn_a_latent",
    )(sink, pxl, pxl, pxl, pxl, pxl, pxl, pxl, pxc, pxc)


def _attn_a_ctx_kernel(sink_ref, q_ref, k_ref, v_ref, o_ref):
    kvh = pl.program_id(1)
    rows = q_ref.shape[0]
    q = _stack_heads(q_ref[...])
    v = v_ref[...]
    sink = _sink_column(sink_ref, kvh, rows)
    s = _qkt(q, k_ref[...])
    m = jnp.maximum(jnp.max(s, axis=-1, keepdims=True), sink)
    p = jnp.exp(s - m)
    denom = jnp.sum(p, axis=-1, keepdims=True) + jnp.exp(sink - m)
    o = jnp.dot(p.astype(v.dtype), v, preferred_element_type=F32) / denom
    o_ref[...] = _unstack_heads(o, A_GROUP).astype(o_ref.dtype)


def _attn_a_ctx(pxc, sink, *, batch, ctx_len, d, kv0_c):
    kvh = d // HEAD_DIM // A_GROUP
    gw = A_GROUP * HEAD_DIM
    ka_c = (kv0_c + 2 * d) // HEAD_DIM
    va_c = ka_c + kvh
    return pl.pallas_call(
        _attn_a_ctx_kernel,
        out_shape=jax.ShapeDtypeStruct((batch * ctx_len, d), BF16),
        grid=(batch, kvh),
        in_specs=[pl.BlockSpec(memory_space=pltpu.SMEM),
                  pl.BlockSpec((ctx_len, gw), lambda b, h: (b, h)),
                  pl.BlockSpec((ctx_len, HEAD_DIM), lambda b, h: (b, ka_c + h)),
                  pl.BlockSpec((ctx_len, HEAD_DIM), lambda b, h: (b, va_c + h))],
        out_specs=pl.BlockSpec((ctx_len, gw), lambda b, h: (b, h)),
        compiler_params=_cparams("parallel", "arbitrary"),
        name="attn_a_context",
    )(sink, pxc, pxc, pxc)


def _diff_finish(o0, l0, o1, l1, lamqk_ref, g_ref, lam_init):
    lq = lamqk_ref[...]
    lam = (jnp.exp(jnp.sum(lq[0:1] * lq[1:2], axis=-1, keepdims=True))
           - jnp.exp(jnp.sum(lq[2:3] * lq[3:4], axis=-1, keepdims=True)) + lam_init)
    o = o0 / l0 - lam * (o1 / l1)
    y = o * lax.rsqrt(jnp.mean(o * o, axis=-1, keepdims=True) + EPS) * g_ref[...]
    return y * (1.0 - lam_init)


def _attn_b_ctx_kernel(q_ref, k_ref, v_ref, lamqk_ref, g_ref, o_ref, *, lam_init):
    v = v_ref[...]
    parts = []
    for mp in range(2):
        cols = slice(mp * HEAD_DIM, (mp + 1) * HEAD_DIM)
        s = _qkt(q_ref[:, cols], k_ref[:, cols])
        p = jnp.exp2(s - jnp.max(s, axis=-1, keepdims=True))
        parts += [jnp.dot(p.astype(v.dtype), v, preferred_element_type=F32),
                  jnp.sum(p, axis=-1, keepdims=True)]
    o_ref[...] = _diff_finish(*parts, lamqk_ref, g_ref, lam_init).astype(o_ref.dtype)


def _attn_b_ctx(pxc, lamqk, g_sub, *, lam_init, batch, ctx_len, d, kv0_c):
    hw = 2 * HEAD_DIM
    heads = d // hw
    qb, kb, vb = (2 * d) // hw, kv0_c // hw, (kv0_c + d) // hw
    return pl.pallas_call(
        functools.partial(_attn_b_ctx_kernel, lam_init=lam_init),
        out_shape=jax.ShapeDtypeStruct((batch * ctx_len, d), BF16),
        grid=(batch, heads),
        in_specs=[pl.BlockSpec((ctx_len, hw), lambda b, h: (b, qb + h)),
                  pl.BlockSpec((ctx_len, hw), lambda b, h: (b, kb + h)),
                  pl.BlockSpec((ctx_len, hw), lambda b, h: (b, vb + h)),
                  pl.BlockSpec((4, HEAD_DIM), lambda b, h: (0, 0)),
                  pl.BlockSpec((1, hw), lambda b, h: (0, 0))],
        out_specs=pl.BlockSpec((ctx_len, hw), lambda b, h: (b, h)),
        compiler_params=_cparams("parallel", "arbitrary"),
        name="attn_b_context",
    )(pxc, pxc, pxc, lamqk, g_sub.reshape(1, hw))


def _attn_b_latent_kernel(q_ref, k_ref, v_ref, kc_ref, vc_ref, lamqk_ref, g_ref, o_ref,
                          sa_scr, sb_scr, pa_scr, pb_scr, ala_scr, alb_scr, m_scr, l_scr, acc_scr,
                          *, lam_init, tk):
    nk = k_ref.shape[0] // tk
    tq = q_ref.shape[0]
    lanes = HEAD_DIM

    def lane_groups(x):
        return [x[:, g * lanes:(g + 1) * lanes] for g in range(x.shape[1] // lanes)]

    def row_max(s):
        mx = functools.reduce(jnp.maximum, lane_groups(s))
        return jnp.broadcast_to(jnp.max(mx, axis=-1, keepdims=True), (tq, lanes))

    def scores(c, s_scr):
        start = pl.multiple_of(c * tk, tk)
        for mp in range(2):
            cols = slice(mp * HEAD_DIM, (mp + 1) * HEAD_DIM)
            s_scr[mp] = _qkt(q_ref[:, cols], k_ref[pl.ds(start, tk), cols])

    def softmax(s_scr, p_scr, al_scr):
        for mp in range(2):
            s = s_scr[mp]
            m_old = m_scr[mp]
            m_new = jnp.maximum(m_old, row_max(s))
            alpha = jnp.exp2(m_old - m_new)
            psum = None
            for g, sg in enumerate(lane_groups(s)):
                pg = jnp.exp2(sg - m_new)
                p_scr[mp, :, g * lanes:(g + 1) * lanes] = pg.astype(p_scr.dtype)
                psum = pg if psum is None else psum + pg
            l_scr[mp] = alpha * l_scr[mp] + psum
            al_scr[mp] = alpha
            m_scr[mp] = m_new

    def weigh(c, p_scr, al_scr):
        v = v_ref[pl.ds(pl.multiple_of(c * tk, tk), tk), :]
        for mp in range(2):
            al = al_scr[mp]
            al = jnp.concatenate([al] * (acc_scr.shape[2] // lanes), axis=1)
            acc_scr[mp] = al * acc_scr[mp] + jnp.dot(p_scr[mp], v, preferred_element_type=F32)

    scores(0, sa_scr)
    vc = vc_ref[...]
    for mp in range(2):
        cols = slice(mp * HEAD_DIM, (mp + 1) * HEAD_DIM)
        s = _qkt(q_ref[:, cols], kc_ref[:, cols])
        m0 = row_max(s)
        pgs = [jnp.exp2(sg - m0) for sg in lane_groups(s)]
        m_scr[mp] = m0
        l_scr[mp] = functools.reduce(jnp.add, pgs)
        acc_scr[mp] = jnp.dot(jnp.concatenate(pgs, axis=1).astype(vc.dtype), vc,
                              preferred_element_type=F32)
    softmax(sa_scr, pa_scr, ala_scr)
    scores(1, sb_scr)

    def pair(i, carry):
        weigh(2 * i, pa_scr, ala_scr)
        softmax(sb_scr, pb_scr, alb_scr)
        scores(2 * i + 2, sa_scr)
        weigh(2 * i + 1, pb_scr, alb_scr)
        softmax(sa_scr, pa_scr, ala_scr)
        scores(2 * i + 3, sb_scr)
        return carry

    lax.fori_loop(0, nk // 2 - 1, pair, 0)
    weigh(nk - 2, pa_scr, ala_scr)
    softmax(sb_scr, pb_scr, alb_scr)
    weigh(nk - 1, pb_scr, alb_scr)

    l0 = jnp.sum(l_scr[0], axis=-1, keepdims=True)
    l1 = jnp.sum(l_scr[1], axis=-1, keepdims=True)
    o_ref[...] = _diff_finish(acc_scr[0], l0, acc_scr[1], l1, lamqk_ref, g_ref,
                              lam_init).astype(o_ref.dtype)


def _attn_b_latent(pxl, pxc, lamqk, g_sub, *, lam_init, batch, seq, ctx_len, d, kv0_l, kv0_c, tq, tk):
    hw = 2 * HEAD_DIM
    heads = d // hw
    nq = seq // tq
    assert (seq // tk) % 2 == 0 and seq // tk >= 2
    qb = (2 * d) // hw
    kb_l, vb_l = kv0_l // hw, (kv0_l + d) // hw
    kb_c, vb_c = kv0_c // hw, (kv0_c + d) // hw
    return pl.pallas_call(
        functools.partial(_attn_b_latent_kernel, lam_init=lam_init, tk=tk),
        out_shape=jax.ShapeDtypeStruct((batch * seq, d), BF16),
        grid=(batch, heads, nq),
        in_specs=[pl.BlockSpec((tq, hw), lambda b, h, i: (b * nq + i, qb + h)),
                  pl.BlockSpec((seq, hw), lambda b, h, i: (b, kb_l + h)),
                  pl.BlockSpec((seq, hw), lambda b, h, i: (b, vb_l + h)),
                  pl.BlockSpec((ctx_len, hw), lambda b, h, i: (b, kb_c + h)),
                  pl.BlockSpec((ctx_len, hw), lambda b, h, i: (b, vb_c + h)),
                  pl.BlockSpec((4, HEAD_DIM), lambda b, h, i: (0, 0)),
                  pl.BlockSpec((1, hw), lambda b, h, i: (0, 0))],
        out_specs=pl.BlockSpec((tq, hw), lambda b, h, i: (b * nq + i, h)),
        scratch_shapes=[pltpu.VMEM((2, tq, tk), F32), pltpu.VMEM((2, tq, tk), F32),
                        pltpu.VMEM((2, tq, tk), BF16), pltpu.VMEM((2, tq, tk), BF16),
                        pltpu.VMEM((2, tq, HEAD_DIM), F32), pltpu.VMEM((2, tq, HEAD_DIM), F32),
                        pltpu.VMEM((2, tq, HEAD_DIM), F32), pltpu.VMEM((2, tq, HEAD_DIM), F32),
                        pltpu.VMEM((2, tq, hw), F32)],
        compiler_params=_cparams("parallel", "parallel", "arbitrary"),
        name="attn_b_latent",
    )(pxl, pxl, pxl, pxc, pxc, lamqk, g_sub.reshape(1, hw))


def _merge1_kernel(oa_ref, za_ref, ob_ref, zb_ref, ga_ref, gb_ref, wa_ref, wb_ref, u_ref,
                   a_scr, b_scr):
    @pl.when(pl.program_id(1) == 0)
    def _():
        a_scr[...] = (oa_ref[...].astype(F32) * _silu(za_ref[...].astype(F32))).astype(a_scr.dtype)
        b_scr[...] = (ob_ref[...].astype(F32) * _silu(zb_ref[...].astype(F32))).astype(b_scr.dtype)

    ya = jnp.dot(a_scr[...], wa_ref[...], preferred_element_type=F32)
    yb = jnp.dot(b_scr[...], wb_ref[...], preferred_element_type=F32)
    u = (jax.nn.sigmoid(ga_ref[...].astype(F32)) * ya + jax.nn.sigmoid(gb_ref[...].astype(F32)) * yb)
    u_ref[...] = u.astype(u_ref.dtype)


def _merge1(o_a, o_b, px, wpa, wpb, *, d, tm, tn):
    t = o_a.shape[0]
    nj = d // tn
    return pl.pallas_call(
        _merge1_kernel,
        out_shape=jax.ShapeDtypeStruct((t, d), BF16),
        grid=(t // tm, nj),
        in_specs=[pl.BlockSpec((tm, d), lambda i, j: (i, 0)),
                  pl.BlockSpec((tm, d), lambda i, j: (i, 1)),
                  pl.BlockSpec((tm, d), lambda i, j: (i, 0)),
                  pl.BlockSpec((tm, d), lambda i, j: (i, 3)),
                  pl.BlockSpec((tm, tn), lambda i, j: (i, 4 * nj + j)),
                  pl.BlockSpec((tm, tn), lambda i, j: (i, 5 * nj + j)),
                  pl.BlockSpec((d, tn), lambda i, j: (0, j)),
                  pl.BlockSpec((d, tn), lambda i, j: (0, j))],
        out_specs=pl.BlockSpec((tm, tn), lambda i, j: (i, j)),
        scratch_shapes=[pltpu.VMEM((tm, d), BF16), pltpu.VMEM((tm, d), BF16)],
        compiler_params=_cparams("parallel", "arbitrary"),
        name="merge_gate",
    )(o_a, px, o_b, px, px, px, wpa, wpb)


def _merge2_kernel(u_ref, wo_ref, x_ref, g_ref, mod_ref, o_ref, *, d):
    y = jnp.dot(u_ref[...], wo_ref[...], preferred_element_type=F32)
    n = y * lax.rsqrt(jnp.mean(y * y, axis=-1, keepdims=True) + EPS) * g_ref[...]
    o_ref[...] = x_ref[...] + mod_ref[:, 2 * d:3 * d] * n


def _merge2(u, wo, x2, g_post, mod3, row_of_tile, tm):
    t, d = x2.shape
    return pl.pallas_call(
        functools.partial(_merge2_kernel, d=d),
        out_shape=jax.ShapeDtypeStruct((t, d), F32),
        grid=(t // tm,),
        in_specs=[pl.BlockSpec((tm, d), lambda i: (i, 0)),
                  pl.BlockSpec((d, d), lambda i: (0, 0)),
                  pl.BlockSpec((tm, d), lambda i: (i, 0)),
                  pl.BlockSpec((1, d), lambda i: (0, 0)),
                  pl.BlockSpec((None, 1, 3 * d), lambda i: (row_of_tile(i), 0, 0))],
        out_specs=pl.BlockSpec((tm, d), lambda i: (i, 0)),
        compiler_params=_cparams("parallel"),
        name="merge_out",
    )(u, wo, x2, g_post.reshape(1, d), mod3)


def _rope_tables(seq):
    rows = seq // GRID_W
    r = jnp.repeat(jnp.arange(rows, dtype=F32), GRID_W)
    col = jnp.tile(jnp.arange(GRID_W, dtype=F32), rows)
    n_freq = HEAD_DIM // 4
    inv = ROPE_THETA ** (-jnp.arange(n_freq, dtype=F32) / n_freq)
    ang = jnp.concatenate([r[:, None] * inv, col[:, None] * inv], axis=-1)
    cos, sin = jnp.cos(ang), jnp.sin(ang)
    return jnp.concatenate([cos, cos], axis=-1), jnp.concatenate([-sin, sin], axis=-1)


def _regroup_w_in(w, d):
    kv = d // A_GROUP
    k_a, v_a = w[:, 0:kv], w[:, kv:2 * kv]
    k_b, v_b = w[:, 2 * kv:2 * kv + d], w[:, 2 * kv + d:2 * kv + 2 * d]
    q = w[:, 2 * kv + 2 * d:]
    return jnp.concatenate([q, k_b, v_b, k_a, v_a], axis=-1).astype(BF16)


def _tile_types(d, tn, rope):
    per = d // tn
    kv = d // A_GROUP // tn
    r = 1 if rope else 0
    seg = [(per, r | 2), (per, 0), (per, r | 2 | 4), (per, 0), (per, 0), (per, 0),
           (per, r), (per, 0), (kv, r), (kv, 0)]
    out = []
    for n, ty in seg:
        out += [ty] * n
    return jnp.asarray(out, jnp.int32)


def kernel(x, c, ctx, c_ctx, w_ada, b_ada, g_pre, g_post, w_in, sink, lam_qk, g_subln,
           w_proj_a, w_proj_b, w_out):
    batch, seq, d = x.shape
    ctx_len = ctx.shape[1]
    depth = w_in.shape[0]
    t_lat, t_ctx = batch * seq, batch * ctx_len
    kv_cols = 2 * d + 2 * (d // A_GROUP)
    q_cols = 6 * d
    in_cols = q_cols + kv_cols

    tn_in = min(512, d // A_GROUP)
    tm_lat = _pick(seq, (1024, 512, 256))
    tm_ctx = _pick(t_ctx, (512, 256))
    tq_a = _pick(seq, (512, 256, 128))
    tq_b = _pick(seq, (1024, 512, 256))
    tk_b = _pick(seq // 2, (512, 256))
    tn_m = _pick(d, (1024, 512))
    tm_m = _pick(seq, (512, 256))

    cosf, sinf = _rope_tables(seq)
    zeros_tab = jnp.zeros((tm_ctx, HEAD_DIM), F32)
    types_lat = _tile_types(d, tn_in, rope=True)
    types_ctx = _tile_types(d, tn_in, rope=False)
    cvec8 = jnp.concatenate([c, c_ctx[None, :], jnp.zeros((8 - batch - 1, d), F32)], axis=0)

    x2 = x.reshape(t_lat, d)
    c2 = ctx.reshape(t_ctx, d)
    lat_row = lambda tm: (lambda i: i // (seq // tm))
    ctx_row = lambda i: batch

    for l in range(depth):
        last = l == depth - 1
        lam_init = 0.8 - 0.6 * math.exp(-0.3 * l)
        w_l = _regroup_w_in(w_in[l], d)
        wpa, wpb, wo = w_proj_a[l].astype(BF16), w_proj_b[l].astype(BF16), w_out[l].astype(BF16)
        mod3 = _ada(cvec8, w_ada[l], b_ada[l]).reshape(8, 1, 3 * d)

        hx = _prenorm(x2, g_pre[l], mod3, lat_row(tm_lat), tm_lat)
        hc = _prenorm(c2, g_pre[l], mod3, ctx_row, tm_ctx)
        pxl = _inproj(hx, w_l, types_lat, cosf, sinf, col0=0, ncols=in_cols,
                      tm=tm_lat, tn=tn_in, pos_tiles=seq // tm_lat)
        if last:
            pxc = _inproj(hc, w_l, types_ctx, zeros_tab, zeros_tab, col0=q_cols, ncols=kv_cols,
                          tm=tm_ctx, tn=tn_in, pos_tiles=1)
            kv0_c = 0
        else:
            pxc = _inproj(hc, w_l, types_ctx, zeros_tab, zeros_tab, col0=0, ncols=in_cols,
                          tm=tm_ctx, tn=tn_in, pos_tiles=1)
            kv0_c = q_cols

        o_a = _attn_a(pxl, pxc, sink[l], batch=batch, seq=seq, ctx_len=ctx_len, d=d,
                      kv0_l=q_cols, kv0_c=kv0_c, tq=tq_a)
        o_b = _attn_b_latent(pxl, pxc, lam_qk[l], g_subln[l], lam_init=lam_init, batch=batch,
                             seq=seq, ctx_len=ctx_len, d=d, kv0_l=q_cols, kv0_c=kv0_c,
                             tq=tq_b, tk=tk_b)
        u = _merge1(o_a, o_b, pxl, wpa, wpb, d=d, tm=tm_m, tn=tn_m)

        if not last:
            oc_a = _attn_a_ctx(pxc, sink[l], batch=batch, ctx_len=ctx_len, d=d, kv0_c=kv0_c)
            oc_b = _attn_b_ctx(pxc, lam_qk[l], g_subln[l], lam_init=lam_init, batch=batch,
                               ctx_len=ctx_len, d=d, kv0_c=kv0_c)
            uc = _merge1(oc_a, oc_b, pxc, wpa, wpb, d=d, tm=tm_ctx, tn=tn_m)
            c2 = _merge2(uc, wo, c2, g_post[l], mod3, ctx_row, tm_ctx)

        x2 = _merge2(u, wo, x2, g_post[l], mod3, lat_row(tm_m), tm_m)

    return x2.reshape(batch, seq, d)
```

```python
import functools
import math

import jax
import jax.numpy as jnp
from jax import lax
from jax.experimental import pallas as pl
from jax.experimental.pallas import tpu as pltpu

HEAD_DIM = 128
GRID_W = 64
WINDOW = 128
A_GROUP = 4
ROPE_THETA = 10000.0
EPS = 1e-6
NEG = -1e30
Q_SCALE = HEAD_DIM ** -0.5
LOG2E = math.log2(math.e)
VMEM_LIMIT_BYTES = 56 * 1024 * 1024

F32 = jnp.float32
BF16 = jnp.bfloat16


def _cparams(*semantics):
    return pltpu.CompilerParams(dimension_semantics=semantics, vmem_limit_bytes=VMEM_LIMIT_BYTES)


def _pick(n, candidates):
    for c in candidates:
        if n % c == 0:
            return c
    raise ValueError(f"no tile in {candidates} divides {n}")


def _silu(v):
    return v * jax.nn.sigmoid(v)


def _ada_kernel(c_ref, w_ref, b_ref, o_ref):
    a = _silu(c_ref[...])
    o_ref[...] = jnp.dot(a, w_ref[...], preferred_element_type=F32,
                         precision=lax.Precision.HIGHEST) + b_ref[...]


def _ada(cvec8, w, b):
    d, n = w.shape
    tn = _pick(n, (1024, 768, 512, 384, 256, 128))
    return pl.pallas_call(
        _ada_kernel,
        out_shape=jax.ShapeDtypeStruct((8, n), F32),
        grid=(n // tn,),
        in_specs=[pl.BlockSpec((8, d), lambda j: (0, 0)),
                  pl.BlockSpec((d, tn), lambda j: (0, j)),
                  pl.BlockSpec((1, tn), lambda j: (0, j))],
        out_specs=pl.BlockSpec((8, tn), lambda j: (0, j)),
        compiler_params=_cparams("arbitrary"),
        name="ada_ln",
    )(cvec8, w, b.reshape(1, n))


def _prenorm_kernel(x_ref, g_ref, mod_ref, o_ref, *, d):
    x = x_ref[...]
    y = x * lax.rsqrt(jnp.mean(x * x, axis=-1, keepdims=True) + EPS)
    y = y * g_ref[...]
    shift = mod_ref[:, 0:d]
    scale = mod_ref[:, d:2 * d]
    o_ref[...] = (y * (1.0 + scale) + shift).astype(o_ref.dtype)


def _prenorm(x2, g, mod3, row_of_tile, tm):
    t, d = x2.shape
    return pl.pallas_call(
        functools.partial(_prenorm_kernel, d=d),
        out_shape=jax.ShapeDtypeStruct((t, d), BF16),
        grid=(t // tm,),
        in_specs=[pl.BlockSpec((tm, d), lambda i: (i, 0)),
                  pl.BlockSpec((1, d), lambda i: (0, 0)),
                  pl.BlockSpec((None, 1, 3 * d), lambda i: (row_of_tile(i), 0, 0))],
        out_specs=pl.BlockSpec((tm, d), lambda i: (i, 0)),
        compiler_params=_cparams("parallel"),
        name="prenorm",
    )(x2, g.reshape(1, d), mod3)


def _inproj_kernel(type_ref, h_ref, w_ref, cos_ref, sin_ref, o_ref, *, tile0):
    t = type_ref[tile0 + pl.program_id(1)]
    heads = o_ref.shape[1] // HEAD_DIM

    def matmul():
        return jnp.dot(h_ref[...], w_ref[...], preferred_element_type=F32)

    @pl.when(t == 0)
    def _():
        o_ref[...] = matmul().astype(o_ref.dtype)

    @pl.when(t != 0)
    def _():
        acc = matmul()
        rope = (t & 1).astype(F32)
        scale = (jnp.where((t & 2) != 0, Q_SCALE, 1.0) * jnp.where((t & 4) != 0, LOG2E, 1.0)).astype(F32)
        ca = (rope * cos_ref[...] + (1.0 - rope)) * scale
        sa = (rope * sin_ref[...]) * scale
        for hh in range(heads):
            a = acc[:, hh * HEAD_DIM:(hh + 1) * HEAD_DIM]
            r = a * ca + pltpu.roll(a, HEAD_DIM // 2, 1) * sa
            o_ref[:, hh * HEAD_DIM:(hh + 1) * HEAD_DIM] = r.astype(o_ref.dtype)


def _inproj(h, w, types, cosf, sinf, *, col0, ncols, tm, tn, pos_tiles):
    t, d = h.shape
    tile0 = col0 // tn
    grid_spec = pltpu.PrefetchScalarGridSpec(
        num_scalar_prefetch=1,
        grid=(t // tm, ncols // tn),
        in_specs=[pl.BlockSpec((tm, d), lambda i, j, ty: (i, 0)),
                  pl.BlockSpec((d, tn), lambda i, j, ty: (0, tile0 + j)),
                  pl.BlockSpec((tm, HEAD_DIM), lambda i, j, ty: (i % pos_tiles, 0)),
                  pl.BlockSpec((tm, HEAD_DIM), lambda i, j, ty: (i % pos_tiles, 0))],
        out_specs=pl.BlockSpec((tm, tn), lambda i, j, ty: (i, j)),
    )
    return pl.pallas_call(
        functools.partial(_inproj_kernel, tile0=tile0),
        out_shape=jax.ShapeDtypeStruct((t, ncols), BF16),
        grid_spec=grid_spec,
        compiler_params=_cparams("parallel", "arbitrary"),
        name="in_proj",
    )(types, h, w, cosf, sinf)


def _stack_heads(q):
    g = q.shape[1] // HEAD_DIM
    return jnp.concatenate([q[:, i * HEAD_DIM:(i + 1) * HEAD_DIM] for i in range(g)], axis=0)


def _unstack_heads(o, g):
    rows = o.shape[0] // g
    return jnp.concatenate([o[i * rows:(i + 1) * rows, :] for i in range(g)], axis=1)


def _sink_column(sink_ref, kvh, rows):
    return jnp.concatenate(
        [jnp.full((rows, 1), sink_ref[kvh * A_GROUP + g], F32) for g in range(A_GROUP)], axis=0)


def _qkt(q, k):
    return lax.dot_general(q, k, (((1,), (1,)), ((), ())), preferred_element_type=F32)


def _attn_a_kernel(sink_ref, q_ref, kp_ref, km_ref, kn_ref, vp_ref, vm_ref, vn_ref, kc_ref, vc_ref,
                   o_ref, *, seq):
    i = pl.program_id(1)
    kvh = pl.program_id(2)
    tq = q_ref.shape[0]
    k_win = jnp.concatenate([kp_ref[...], km_ref[...], kn_ref[...]], axis=0)
    v_win = jnp.concatenate([vp_ref[...], vm_ref[...], vn_ref[...]], axis=0)
    kc = kc_ref[...]
    vc = vc_ref[...]
    rows = A_GROUP * WINDOW
    nloc = 3 * WINDOW
    nkeys = nloc + kc.shape[0]
    sink = _sink_column(sink_ref, kvh, WINDOW) * LOG2E
    qi = lax.broadcasted_iota(jnp.int32, (rows, nkeys), 0) & (WINDOW - 1)
    kj = lax.broadcasted_iota(jnp.int32, (rows, nkeys), 1)
    band = jnp.abs(qi + WINDOW - kj) <= WINDOW
    is_ctx = kj >= nloc
    nsub = tq // WINDOW

    def scores(r):
        q = _stack_heads(q_ref[r * WINDOW:(r + 1) * WINDOW, :])
        k = jnp.concatenate([k_win[r * WINDOW:(r + 3) * WINDOW, :], kc], axis=0)
        base = i * tq + (r - 1) * WINDOW
        valid = is_ctx | (band & (kj >= -base) & (kj < seq - base))
        return jnp.where(valid, _qkt(q, k), NEG)

    def lane_bcast(col):
        return jnp.broadcast_to(col, (rows, HEAD_DIM))

    sink_b = lane_bcast(sink)

    def softmax(s):
        groups = [s[:, g * HEAD_DIM:(g + 1) * HEAD_DIM] for g in range(nkeys // HEAD_DIM)]
        gmax = functools.reduce(jnp.maximum, groups)
        m = jnp.maximum(lane_bcast(jnp.max(gmax, axis=-1, keepdims=True)), sink_b)
        ps = [jnp.exp2(sg - m) for sg in groups]
        psum = functools.reduce(jnp.add, ps)
        denom = lane_bcast(jnp.sum(psum, axis=-1, keepdims=True)) + jnp.exp2(sink_b - m)
        return jnp.concatenate([pg.astype(vc.dtype) for pg in ps], axis=1), denom

    def weigh(r, p, denom):
        v = jnp.concatenate([v_win[r * WINDOW:(r + 3) * WINDOW, :], vc], axis=0)
        o = jnp.dot(p, v, preferred_element_type=F32) / denom
        o_ref[r * WINDOW:(r + 1) * WINDOW, :] = _unstack_heads(o, A_GROUP).astype(o_ref.dtype)

    s_next = scores(0)
    pd_next = None
    for r in range(nsub + 2):
        pd_cur, pd_next = pd_next, None
        s_cur, s_next = s_next, None
        if r < nsub - 1:
            s_next = scores(r + 1)
        if s_cur is not None and r < nsub:
            pd_next = softmax(s_cur)
        if pd_cur is not None:
            weigh(r - 1, *pd_cur)


def _attn_a(pxl, pxc, sink, *, batch, seq, ctx_len, d, kv0_l, kv0_c, tq):
    kvh = d // HEAD_DIM // A_GROUP
    gw = A_GROUP * HEAD_DIM
    nq = seq // tq
    r = tq // WINDOW
    nblk = seq // WINDOW
    ka_l = (kv0_l + 2 * d) // HEAD_DIM
    va_l = ka_l + kvh
    ka_c = (kv0_c + 2 * d) // HEAD_DIM
    va_c = ka_c + kvh

    def prev(b, i, h):
        return (b * nblk + jnp.maximum(i * r - 1, 0), h)

    def nxt(b, i, h):
        return (b * nblk + jnp.minimum((i + 1) * r, nblk - 1), h)

    halo = (WINDOW, HEAD_DIM)
    return pl.pallas_call(
        functools.partial(_attn_a_kernel, seq=seq),
        out_shape=jax.ShapeDtypeStruct((batch * seq, d), BF16),
        grid=(batch, nq, kvh),
        in_specs=[
            pl.BlockSpec(memory_space=pltpu.SMEM),
            pl.BlockSpec((tq, gw), lambda b, i, h: (b * nq + i, h)),
            pl.BlockSpec(halo, lambda b, i, h: prev(b, i, ka_l + h)),
            pl.BlockSpec((tq, HEAD_DIM), lambda b, i, h: (b * nq + i, ka_l + h)),
            pl.BlockSpec(halo, lambda b, i, h: nxt(b, i, ka_l + h)),
            pl.BlockSpec(halo, lambda b, i, h: prev(b, i, va_l + h)),
            pl.BlockSpec((tq, HEAD_DIM), lambda b, i, h: (b * nq + i, va_l + h)),
            pl.BlockSpec(halo, lambda b, i, h: nxt(b, i, va_l + h)),
            pl.BlockSpec((ctx_len, HEAD_DIM), lambda b, i, h: (b, ka_c + h)),
            pl.BlockSpec((ctx_len, HEAD_DIM), lambda b, i, h: (b, va_c + h)),
        ],
        out_specs=pl.BlockSpec((tq, gw), lambda b, i, h: (b * nq + i, h)),
        compiler_params=_cparams("parallel", "parallel", "arbitrary"),
        name="attn_a_latent",
    )(sink, pxl, pxl, pxl, pxl, pxl, pxl, pxl, pxc, pxc)


def _attn_a_ctx_kernel(sink_ref, q_ref, k_ref, v_ref, o_ref):
    kvh = pl.program_id(1)
    rows = q_ref.shape[0]
    q = _stack_heads(q_ref[...])
    v = v_ref[...]
    sink = _sink_column(sink_ref, kvh, rows) * LOG2E
    s = _qkt(q, k_ref[...])
    m = jnp.maximum(jnp.max(s, axis=-1, keepdims=True), sink)
    p = jnp.exp2(s - m)
    denom = jnp.sum(p, axis=-1, keepdims=True) + jnp.exp2(sink - m)
    o = jnp.dot(p.astype(v.dtype), v, preferred_element_type=F32) / denom
    o_ref[...] = _unstack_heads(o, A_GROUP).astype(o_ref.dtype)


def _attn_a_ctx(pxc, sink, *, batch, ctx_len, d, kv0_c):
    kvh = d // HEAD_DIM // A_GROUP
    gw = A_GROUP * HEAD_DIM
    ka_c = (kv0_c + 2 * d) // HEAD_DIM
    va_c = ka_c + kvh
    return pl.pallas_call(
        _attn_a_ctx_kernel,
        out_shape=jax.ShapeDtypeStruct((batch * ctx_len, d), BF16),
        grid=(batch, kvh),
        in_specs=[pl.BlockSpec(memory_space=pltpu.SMEM),
                  pl.BlockSpec((ctx_len, gw), lambda b, h: (b, h)),
                  pl.BlockSpec((ctx_len, HEAD_DIM), lambda b, h: (b, ka_c + h)),
                  pl.BlockSpec((ctx_len, HEAD_DIM), lambda b, h: (b, va_c + h))],
        out_specs=pl.BlockSpec((ctx_len, gw), lambda b, h: (b, h)),
        compiler_params=_cparams("parallel", "arbitrary"),
        name="attn_a_context",
    )(sink, pxc, pxc, pxc)


def _row_sum_lanes(x):
    groups = [x[:, g * HEAD_DIM:(g + 1) * HEAD_DIM] for g in range(x.shape[1] // HEAD_DIM)]
    part = functools.reduce(jnp.add, groups)
    return jnp.broadcast_to(jnp.sum(part, axis=-1, keepdims=True), part.shape)


def _diff_finish(o0, l0, o1, l1, lamqk_ref, g_ref, lam_init):
    lq = lamqk_ref[...]
    lam = (jnp.exp(jnp.sum(lq[0:1] * lq[1:2], axis=-1, keepdims=True))
           - jnp.exp(jnp.sum(lq[2:3] * lq[3:4], axis=-1, keepdims=True)) + lam_init)
    wide = lambda t: jnp.concatenate([t] * (o0.shape[1] // HEAD_DIM), axis=1)
    o = o0 / wide(l0) - lam * (o1 / wide(l1))
    inv = lax.rsqrt(_row_sum_lanes(o * o) * (1.0 / o.shape[1]) + EPS)
    return o * wide(inv) * g_ref[...] * (1.0 - lam_init)


def _attn_b_ctx_kernel(q_ref, k_ref, v_ref, lamqk_ref, g_ref, o_ref, *, lam_init):
    v = v_ref[...]
    parts = []
    for mp in range(2):
        cols = slice(mp * HEAD_DIM, (mp + 1) * HEAD_DIM)
        s = _qkt(q_ref[:, cols], k_ref[:, cols])
        p = jnp.exp2(s - jnp.max(s, axis=-1, keepdims=True))
        parts += [jnp.dot(p.astype(v.dtype), v, preferred_element_type=F32), _row_sum_lanes(p)]
    o_ref[...] = _diff_finish(*parts, lamqk_ref, g_ref, lam_init).astype(o_ref.dtype)


def _attn_b_ctx(pxc, lamqk, g_sub, *, lam_init, batch, ctx_len, d, kv0_c):
    hw = 2 * HEAD_DIM
    heads = d // hw
    qb, kb, vb = (2 * d) // hw, kv0_c // hw, (kv0_c + d) // hw
    return pl.pallas_call(
        functools.partial(_attn_b_ctx_kernel, lam_init=lam_init),
        out_shape=jax.ShapeDtypeStruct((batch * ctx_len, d), BF16),
        grid=(batch, heads),
        in_specs=[pl.BlockSpec((ctx_len, hw), lambda b, h: (b, qb + h)),
                  pl.BlockSpec((ctx_len, hw), lambda b, h: (b, kb + h)),
                  pl.BlockSpec((ctx_len, hw), lambda b, h: (b, vb + h)),
                  pl.BlockSpec((4, HEAD_DIM), lambda b, h: (0, 0)),
                  pl.BlockSpec((1, hw), lambda b, h: (0, 0))],
        out_specs=pl.BlockSpec((ctx_len, hw), lambda b, h: (b, h)),
        compiler_params=_cparams("parallel", "arbitrary"),
        name="attn_b_context",
    )(pxc, pxc, pxc, lamqk, g_sub.reshape(1, hw))


def _attn_b_latent_kernel(q_ref, k_ref, v_ref, kc_ref, vc_ref, lamqk_ref, g_ref, o_ref,
                          sa_scr, sb_scr, xa_scr, xb_scr, pa_scr, pb_scr, ala_scr, alb_scr,
                          m_scr, l_scr, acc_scr, *, lam_init, tk):
    nk = k_ref.shape[0] // tk
    tq = q_ref.shape[0]
    lanes = HEAD_DIM

    def lane_groups(x):
        return [x[:, g * lanes:(g + 1) * lanes] for g in range(x.shape[1] // lanes)]

    def group_max(s):
        return functools.reduce(jnp.maximum, lane_groups(s))

    def row_max(gmax):
        return jnp.broadcast_to(jnp.max(gmax, axis=-1, keepdims=True), (tq, lanes))

    def scores(c, buf):
        s_scr, x_scr = buf[0], buf[1]
        start = pl.multiple_of(c * tk, tk)
        for mp in range(2):
            cols = slice(mp * HEAD_DIM, (mp + 1) * HEAD_DIM)
            s = _qkt(q_ref[:, cols], k_ref[pl.ds(start, tk), cols])
            s_scr[mp] = s
            x_scr[mp] = group_max(s)

    def softmax(buf):
        s_scr, x_scr, p_scr, al_scr = buf
        for mp in range(2):
            s = s_scr[mp]
            m_old = m_scr[mp]
            m_new = jnp.maximum(m_old, row_max(x_scr[mp]))
            alpha = jnp.exp2(m_old - m_new)
            psum = None
            for g, sg in enumerate(lane_groups(s)):
                pg = jnp.exp2(sg - m_new)
                p_scr[mp, :, g * lanes:(g + 1) * lanes] = pg.astype(p_scr.dtype)
                psum = pg if psum is None else psum + pg
            l_scr[mp] = alpha * l_scr[mp] + psum
            al_scr[mp] = alpha
            m_scr[mp] = m_new

    def weigh(c, buf):
        p_scr, al_scr = buf[2], buf[3]
        v = v_ref[pl.ds(pl.multiple_of(c * tk, tk), tk), :]
        for mp in range(2):
            al = al_scr[mp]
            al = jnp.concatenate([al] * (acc_scr.shape[2] // lanes), axis=1)
            acc_scr[mp] = al * acc_scr[mp] + jnp.dot(p_scr[mp], v, preferred_element_type=F32)

    buf_a = (sa_scr, xa_scr, pa_scr, ala_scr)
    buf_b = (sb_scr, xb_scr, pb_scr, alb_scr)
    scores(0, buf_a)
    vc = vc_ref[...]
    for mp in range(2):
        cols = slice(mp * HEAD_DIM, (mp + 1) * HEAD_DIM)
        s = _qkt(q_ref[:, cols], kc_ref[:, cols])
        m0 = row_max(group_max(s))
        pgs = [jnp.exp2(sg - m0) for sg in lane_groups(s)]
        m_scr[mp] = m0
        l_scr[mp] = functools.reduce(jnp.add, pgs)
        acc_scr[mp] = jnp.dot(jnp.concatenate(pgs, axis=1).astype(vc.dtype), vc,
                              preferred_element_type=F32)
    softmax(buf_a)
    scores(1, buf_b)

    def pair(i, carry):
        weigh(2 * i, buf_a)
        softmax(buf_b)
        scores(2 * i + 2, buf_a)
        weigh(2 * i + 1, buf_b)
        softmax(buf_a)
        scores(2 * i + 3, buf_b)
        return carry

    lax.fori_loop(0, nk // 2 - 1, pair, 0)
    weigh(nk - 2, buf_a)
    softmax(buf_b)
    weigh(nk - 1, buf_b)

    o_ref[...] = _diff_finish(acc_scr[0], _row_sum_lanes(l_scr[0]), acc_scr[1],
                              _row_sum_lanes(l_scr[1]), lamqk_ref, g_ref,
                              lam_init).astype(o_ref.dtype)


def _attn_b_latent(pxl, pxc, lamqk, g_sub, *, lam_init, batch, seq, ctx_len, d, kv0_l, kv0_c, tq, tk):
    hw = 2 * HEAD_DIM
    heads = d // hw
    nq = seq // tq
    assert (seq // tk) % 2 == 0 and seq // tk >= 2
    qb = (2 * d) // hw
    kb_l, vb_l = kv0_l // hw, (kv0_l + d) // hw
    kb_c, vb_c = kv0_c // hw, (kv0_c + d) // hw
    return pl.pallas_call(
        functools.partial(_attn_b_latent_kernel, lam_init=lam_init, tk=tk),
        out_shape=jax.ShapeDtypeStruct((batch * seq, d), BF16),
        grid=(batch, heads, nq),
        in_specs=[pl.BlockSpec((tq, hw), lambda b, h, i: (b * nq + i, qb + h)),
                  pl.BlockSpec((seq, hw), lambda b, h, i: (b, kb_l + h)),
                  pl.BlockSpec((seq, hw), lambda b, h, i: (b, vb_l + h)),
                  pl.BlockSpec((ctx_len, hw), lambda b, h, i: (b, kb_c + h)),
                  pl.BlockSpec((ctx_len, hw), lambda b, h, i: (b, vb_c + h)),
                  pl.BlockSpec((4, HEAD_DIM), lambda b, h, i: (0, 0)),
                  pl.BlockSpec((1, hw), lambda b, h, i: (0, 0))],
        out_specs=pl.BlockSpec((tq, hw), lambda b, h, i: (b * nq + i, h)),
        scratch_shapes=[pltpu.VMEM((2, tq, tk), F32), pltpu.VMEM((2, tq, tk), F32),
                        pltpu.VMEM((2, tq, HEAD_DIM), F32), pltpu.VMEM((2, tq, HEAD_DIM), F32),
                        pltpu.VMEM((2, tq, tk), BF16), pltpu.VMEM((2, tq, tk), BF16),
                        pltpu.VMEM((2, tq, HEAD_DIM), F32), pltpu.VMEM((2, tq, HEAD_DIM), F32),
                        pltpu.VMEM((2, tq, HEAD_DIM), F32), pltpu.VMEM((2, tq, HEAD_DIM), F32),
                        pltpu.VMEM((2, tq, hw), F32)],
        compiler_params=_cparams("parallel", "parallel", "arbitrary"),
        name="attn_b_latent",
    )(pxl, pxl, pxl, pxc, pxc, lamqk, g_sub.reshape(1, hw))


def _merge1_kernel(oa_ref, za_ref, ob_ref, zb_ref, ga_ref, gb_ref, wa_ref, wb_ref, u_ref,
                   a_scr, b_scr):
    @pl.when(pl.program_id(1) == 0)
    def _():
        a_scr[...] = (oa_ref[...].astype(F32) * _silu(za_ref[...].astype(F32))).astype(a_scr.dtype)
        b_scr[...] = (ob_ref[...].astype(F32) * _silu(zb_ref[...].astype(F32))).astype(b_scr.dtype)

    ya = jnp.dot(a_scr[...], wa_ref[...], preferred_element_type=F32)
    yb = jnp.dot(b_scr[...], wb_ref[...], preferred_element_type=F32)
    u = (jax.nn.sigmoid(ga_ref[...].astype(F32)) * ya + jax.nn.sigmoid(gb_ref[...].astype(F32)) * yb)
    u_ref[...] = u.astype(u_ref.dtype)


def _merge1(o_a, o_b, px, wpa, wpb, *, d, tm, tn):
    t = o_a.shape[0]
    nj = d // tn
    return pl.pallas_call(
        _merge1_kernel,
        out_shape=jax.ShapeDtypeStruct((t, d), BF16),
        grid=(t // tm, nj),
        in_specs=[pl.BlockSpec((tm, d), lambda i, j: (i, 0)),
                  pl.BlockSpec((tm, d), lambda i, j: (i, 1)),
                  pl.BlockSpec((tm, d), lambda i, j: (i, 0)),
                  pl.BlockSpec((tm, d), lambda i, j: (i, 3)),
                  pl.BlockSpec((tm, tn), lambda i, j: (i, 4 * nj + j)),
                  pl.BlockSpec((tm, tn), lambda i, j: (i, 5 * nj + j)),
                  pl.BlockSpec((d, tn), lambda i, j: (0, j)),
                  pl.BlockSpec((d, tn), lambda i, j: (0, j))],
        out_specs=pl.BlockSpec((tm, tn), lambda i, j: (i, j)),
        scratch_shapes=[pltpu.VMEM((tm, d), BF16), pltpu.VMEM((tm, d), BF16)],
        compiler_params=_cparams("parallel", "arbitrary"),
        name="merge_gate",
    )(o_a, px, o_b, px, px, px, wpa, wpb)


def _merge2_kernel(u_ref, wo_ref, x_ref, g_ref, mod_ref, o_ref, *, d):
    y = jnp.dot(u_ref[...], wo_ref[...], preferred_element_type=F32)
    n = y * lax.rsqrt(jnp.mean(y * y, axis=-1, keepdims=True) + EPS) * g_ref[...]
    o_ref[...] = x_ref[...] + mod_ref[:, 2 * d:3 * d] * n


def _merge2(u, wo, x2, g_post, mod3, row_of_tile, tm):
    t, d = x2.shape
    return pl.pallas_call(
        functools.partial(_merge2_kernel, d=d),
        out_shape=jax.ShapeDtypeStruct((t, d), F32),
        grid=(t // tm,),
        in_specs=[pl.BlockSpec((tm, d), lambda i: (i, 0)),
                  pl.BlockSpec((d, d), lambda i: (0, 0)),
                  pl.BlockSpec((tm, d), lambda i: (i, 0)),
                  pl.BlockSpec((1, d), lambda i: (0, 0)),
                  pl.BlockSpec((None, 1, 3 * d), lambda i: (row_of_tile(i), 0, 0))],
        out_specs=pl.BlockSpec((tm, d), lambda i: (i, 0)),
        compiler_params=_cparams("parallel"),
        name="merge_out",
    )(u, wo, x2, g_post.reshape(1, d), mod3)


def _rope_tables(seq):
    rows = seq // GRID_W
    r = jnp.repeat(jnp.arange(rows, dtype=F32), GRID_W)
    col = jnp.tile(jnp.arange(GRID_W, dtype=F32), rows)
    n_freq = HEAD_DIM // 4
    inv = ROPE_THETA ** (-jnp.arange(n_freq, dtype=F32) / n_freq)
    ang = jnp.concatenate([r[:, None] * inv, col[:, None] * inv], axis=-1)
    cos, sin = jnp.cos(ang), jnp.sin(ang)
    return jnp.concatenate([cos, cos], axis=-1), jnp.concatenate([-sin, sin], axis=-1)


def _regroup_w_in(w, d):
    kv = d // A_GROUP
    k_a, v_a = w[:, 0:kv], w[:, kv:2 * kv]
    k_b, v_b = w[:, 2 * kv:2 * kv + d], w[:, 2 * kv + d:2 * kv + 2 * d]
    q = w[:, 2 * kv + 2 * d:]
    return jnp.concatenate([q, k_b, v_b, k_a, v_a], axis=-1).astype(BF16)


def _tile_types(d, tn, rope):
    per = d // tn
    kv = d // A_GROUP // tn
    r = 1 if rope else 0
    seg = [(per, r | 2 | 4), (per, 0), (per, r | 2 | 4), (per, 0), (per, 0), (per, 0),
           (per, r), (per, 0), (kv, r), (kv, 0)]
    out = []
    for n, ty in seg:
        out += [ty] * n
    return jnp.asarray(out, jnp.int32)


def kernel(x, c, ctx, c_ctx, w_ada, b_ada, g_pre, g_post, w_in, sink, lam_qk, g_subln,
           w_proj_a, w_proj_b, w_out):
    batch, seq, d = x.shape
    ctx_len = ctx.shape[1]
    depth = w_in.shape[0]
    t_lat, t_ctx = batch * seq, batch * ctx_len
    kv_cols = 2 * d + 2 * (d // A_GROUP)
    q_cols = 6 * d
    in_cols = q_cols + kv_cols

    tn_in = min(512, d // A_GROUP)
    tm_lat = _pick(seq, (1024, 512, 256))
    tm_ctx = _pick(t_ctx, (512, 256))
    tq_a = _pick(seq, (512, 256, 128))
    tq_b = _pick(seq, (1024, 512, 256))
    tk_b = _pick(seq // 2, (512, 256))
    tn_m = _pick(d, (1024, 512))
    tm_m = _pick(seq, (512, 256))

    cosf, sinf = _rope_tables(seq)
    zeros_tab = jnp.zeros((tm_ctx, HEAD_DIM), F32)
    types_lat = _tile_types(d, tn_in, rope=True)
    types_ctx = _tile_types(d, tn_in, rope=False)
    cvec8 = jnp.concatenate([c, c_ctx[None, :], jnp.zeros((8 - batch - 1, d), F32)], axis=0)

    x2 = x.reshape(t_lat, d)
    c2 = ctx.reshape(t_ctx, d)
    lat_row = lambda tm: (lambda i: i // (seq // tm))
    ctx_row = lambda i: batch

    for l in range(depth):
        last = l == depth - 1
        lam_init = 0.8 - 0.6 * math.exp(-0.3 * l)
        w_l = _regroup_w_in(w_in[l], d)
        wpa, wpb, wo = w_proj_a[l].astype(BF16), w_proj_b[l].astype(BF16), w_out[l].astype(BF16)
        mod3 = _ada(cvec8, w_ada[l], b_ada[l]).reshape(8, 1, 3 * d)

        hx = _prenorm(x2, g_pre[l], mod3, lat_row(tm_lat), tm_lat)
        hc = _prenorm(c2, g_pre[l], mod3, ctx_row, tm_ctx)
        pxl = _inproj(hx, w_l, types_lat, cosf, sinf, col0=0, ncols=in_cols,
                      tm=tm_lat, tn=tn_in, pos_tiles=seq // tm_lat)
        if last:
            pxc = _inproj(hc, w_l, types_ctx, zeros_tab, zeros_tab, col0=q_cols, ncols=kv_cols,
                          tm=tm_ctx, tn=tn_in, pos_tiles=1)
            kv0_c = 0
        else:
            pxc = _inproj(hc, w_l, types_ctx, zeros_tab, zeros_tab, col0=0, ncols=in_cols,
                          tm=tm_ctx, tn=tn_in, pos_tiles=1)
            kv0_c = q_cols

        o_a = _attn_a(pxl, pxc, sink[l], batch=batch, seq=seq, ctx_len=ctx_len, d=d,
                      kv0_l=q_cols, kv0_c=kv0_c, tq=tq_a)
        o_b = _attn_b_latent(pxl, pxc, lam_qk[l], g_subln[l], lam_init=lam_init, batch=batch,
                             seq=seq, ctx_len=ctx_len, d=d, kv0_l=q_cols, kv0_c=kv0_c,
                             tq=tq_b, tk=tk_b)
        u = _merge1(o_a, o_b, pxl, wpa, wpb, d=d, tm=tm_m, tn=tn_m)

        if not last:
            oc_a = _attn_a_ctx(pxc, sink[l], batch=batch, ctx_len=ctx_len, d=d, kv0_c=kv0_c)
            oc_b = _attn_b_ctx(pxc, lam_qk[l], g_subln[l], lam_init=lam_init, batch=batch,
                               ctx_len=ctx_len, d=d, kv0_c=kv0_c)
            uc = _merge1(oc_a, oc_b, pxc, wpa, wpb, d=d, tm=tm_ctx, tn=tn_m)
            c2 = _merge2(uc, wo, c2, g_post[l], mod3, ctx_row, tm_ctx)

        x2 = _merge2(u, wo, x2, g_post[l], mod3, lat_row(tm_m), tm_m)

    return x2.reshape(batch, seq, d)
```

```python
import functools
import math

import jax
import jax.numpy as jnp
from jax import lax
from jax.experimental import pallas as pl
from jax.experimental.pallas import tpu as pltpu

HEAD_DIM = 128
GRID_W = 64
WINDOW = 128
A_GROUP = 4
ROPE_THETA = 10000.0
EPS = 1e-6
NEG = -1e30
Q_SCALE = HEAD_DIM ** -0.5
LOG2E = math.log2(math.e)
VMEM_LIMIT_BYTES = 56 * 1024 * 1024

F32 = jnp.float32
BF16 = jnp.bfloat16


def _cparams(*semantics):
    return pltpu.CompilerParams(dimension_semantics=semantics, vmem_limit_bytes=VMEM_LIMIT_BYTES)


def _pick(n, candidates):
    for c in candidates:
        if n % c == 0:
            return c
    raise ValueError(f"no tile in {candidates} divides {n}")


def _sigmoid(v):
    return 0.5 * jnp.tanh(0.5 * v) + 0.5


def _silu(v):
    h = 0.5 * v
    return h * jnp.tanh(h) + h


def _gate_out(o, z, dtype):
    return (o * _silu(z)).astype(dtype)


def _col_offsets(d):
    kv = d // A_GROUP
    names = ("k_a", "v_a", "k_b", "v_b", "q_a", "z_a", "q_b", "z_b", "g_a", "g_b")
    widths = (kv, kv, d, d, d, d, d, d, d, d)
    off, out = 0, {}
    for n, w in zip(names, widths):
        out[n] = off
        off += w
    out["kv_cols"] = out["q_a"]
    out["in_cols"] = off
    return out


def _ada_kernel(c_ref, w_ref, b_ref, o_ref):
    a = _silu(c_ref[...])
    o_ref[...] = jnp.dot(a, w_ref[...], preferred_element_type=F32,
                         precision=lax.Precision.HIGHEST) + b_ref[...]


def _ada(cvec8, w_all, b_all, layer):
    depth, d, n = w_all.shape
    tn = _pick(n, (1024, 768, 512, 384, 256, 128))
    return pl.pallas_call(
        _ada_kernel,
        out_shape=jax.ShapeDtypeStruct((8, n), F32),
        grid=(n // tn,),
        in_specs=[pl.BlockSpec((8, d), lambda j: (0, 0)),
                  pl.BlockSpec((None, d, tn), lambda j: (layer, 0, j)),
                  pl.BlockSpec((None, 1, tn), lambda j: (layer, 0, j))],
        out_specs=pl.BlockSpec((8, tn), lambda j: (0, j)),
        compiler_params=_cparams("arbitrary"),
        name="ada_ln",
    )(cvec8, w_all, b_all.reshape(depth, 1, n))


def _prenorm_kernel(x_ref, g_ref, mod_ref, o_ref, *, d):
    x = x_ref[...]
    y = x * lax.rsqrt(jnp.mean(x * x, axis=-1, keepdims=True) + EPS)
    y = y * g_ref[...]
    shift = mod_ref[:, 0:d]
    scale = mod_ref[:, d:2 * d]
    o_ref[...] = (y * (1.0 + scale) + shift).astype(o_ref.dtype)


def _prenorm(x2, g, mod3, row_of_tile, tm):
    t, d = x2.shape
    return pl.pallas_call(
        functools.partial(_prenorm_kernel, d=d),
        out_shape=jax.ShapeDtypeStruct((t, d), BF16),
        grid=(t // tm,),
        in_specs=[pl.BlockSpec((tm, d), lambda i: (i, 0)),
                  pl.BlockSpec((1, d), lambda i: (0, 0)),
                  pl.BlockSpec((None, 1, 3 * d), lambda i: (row_of_tile(i), 0, 0))],
        out_specs=pl.BlockSpec((tm, d), lambda i: (i, 0)),
        compiler_params=_cparams("parallel"),
        name="prenorm",
    )(x2, g.reshape(1, d), mod3)


def _inproj_kernel(type_ref, h_ref, w_ref, cos_ref, sin_ref, o_ref):
    t = type_ref[pl.program_id(1)]
    heads = o_ref.shape[1] // HEAD_DIM

    def matmul():
        return jnp.dot(h_ref[...], w_ref[...], preferred_element_type=F32)

    @pl.when(t == 0)
    def _():
        o_ref[...] = matmul().astype(o_ref.dtype)

    @pl.when(t != 0)
    def _():
        acc = matmul()
        rope = (t & 1).astype(F32)
        scale = (jnp.where((t & 2) != 0, Q_SCALE, 1.0) * jnp.where((t & 4) != 0, LOG2E, 1.0)).astype(F32)
        ca = (rope * cos_ref[...] + (1.0 - rope)) * scale
        sa = (rope * sin_ref[...]) * scale
        for hh in range(heads):
            a = acc[:, hh * HEAD_DIM:(hh + 1) * HEAD_DIM]
            r = a * ca + pltpu.roll(a, HEAD_DIM // 2, 1) * sa
            o_ref[:, hh * HEAD_DIM:(hh + 1) * HEAD_DIM] = r.astype(o_ref.dtype)


def _inproj(h, w, types, cosf, sinf, *, ncols, tm, tn, pos_tiles):
    t, d = h.shape
    grid_spec = pltpu.PrefetchScalarGridSpec(
        num_scalar_prefetch=1,
        grid=(t // tm, ncols // tn),
        in_specs=[pl.BlockSpec((tm, d), lambda i, j, ty: (i, 0)),
                  pl.BlockSpec((d, tn), lambda i, j, ty: (0, j)),
                  pl.BlockSpec((tm, HEAD_DIM), lambda i, j, ty: (i % pos_tiles, 0)),
                  pl.BlockSpec((tm, HEAD_DIM), lambda i, j, ty: (i % pos_tiles, 0))],
        out_specs=pl.BlockSpec((tm, tn), lambda i, j, ty: (i, j)),
    )
    return pl.pallas_call(
        _inproj_kernel,
        out_shape=jax.ShapeDtypeStruct((t, ncols), BF16),
        grid_spec=grid_spec,
        compiler_params=_cparams("parallel", "arbitrary"),
        name="in_proj",
    )(types, h, w, cosf, sinf)


def _stack_heads(q):
    g = q.shape[1] // HEAD_DIM
    return jnp.concatenate([q[:, i * HEAD_DIM:(i + 1) * HEAD_DIM] for i in range(g)], axis=0)


def _unstack_heads(o, g):
    rows = o.shape[0] // g
    return jnp.concatenate([o[i * rows:(i + 1) * rows, :] for i in range(g)], axis=1)


def _sink_column(sink_ref, kvh, rows):
    return jnp.concatenate(
        [jnp.full((rows, 1), sink_ref[kvh * A_GROUP + g], F32) for g in range(A_GROUP)], axis=0)


def _qkt(q, k):
    return lax.dot_general(q, k, (((1,), (1,)), ((), ())), preferred_element_type=F32)


def _attn_a_kernel(sink_ref, q_ref, kp_ref, km_ref, kn_ref, vp_ref, vm_ref, vn_ref, kc_ref, vc_ref,
                   z_ref, o_ref, *, seq):
    i = pl.program_id(1)
    kvh = pl.program_id(2)
    tq = q_ref.shape[0]
    k_win = jnp.concatenate([kp_ref[...], km_ref[...], kn_ref[...]], axis=0)
    v_win = jnp.concatenate([vp_ref[...], vm_ref[...], vn_ref[...]], axis=0)
    kc = kc_ref[...]
    vc = vc_ref[...]
    rows = A_GROUP * WINDOW
    nloc = 3 * WINDOW
    nkeys = nloc + kc.shape[0]
    sink = _sink_column(sink_ref, kvh, WINDOW) * LOG2E
    qi = lax.broadcasted_iota(jnp.int32, (rows, nkeys), 0) & (WINDOW - 1)
    kj = lax.broadcasted_iota(jnp.int32, (rows, nkeys), 1)
    band = jnp.abs(qi + WINDOW - kj) <= WINDOW
    is_ctx = kj >= nloc
    nsub = tq // WINDOW

    def scores(r):
        q = _stack_heads(q_ref[r * WINDOW:(r + 1) * WINDOW, :])
        k = jnp.concatenate([k_win[r * WINDOW:(r + 3) * WINDOW, :], kc], axis=0)
        base = i * tq + (r - 1) * WINDOW
        valid = is_ctx | (band & (kj >= -base) & (kj < seq - base))
        return jnp.where(valid, _qkt(q, k), NEG)

    def lane_bcast(col):
        return jnp.broadcast_to(col, (rows, HEAD_DIM))

    sink_b = lane_bcast(sink)

    def softmax(s):
        groups = [s[:, g * HEAD_DIM:(g + 1) * HEAD_DIM] for g in range(nkeys // HEAD_DIM)]
        gmax = functools.reduce(jnp.maximum, groups)
        m = jnp.maximum(lane_bcast(jnp.max(gmax, axis=-1, keepdims=True)), sink_b)
        ps = [jnp.exp2(sg - m) for sg in groups]
        psum = functools.reduce(jnp.add, ps)
        denom = lane_bcast(jnp.sum(psum, axis=-1, keepdims=True)) + jnp.exp2(sink_b - m)
        return jnp.concatenate([pg.astype(vc.dtype) for pg in ps], axis=1), denom

    def weigh(r, p, denom):
        v = jnp.concatenate([v_win[r * WINDOW:(r + 3) * WINDOW, :], vc], axis=0)
        o = _unstack_heads(jnp.dot(p, v, preferred_element_type=F32) / denom, A_GROUP)
        z = z_ref[r * WINDOW:(r + 1) * WINDOW, :].astype(F32)
        o_ref[r * WINDOW:(r + 1) * WINDOW, :] = _gate_out(o, z, o_ref.dtype)

    s_next = scores(0)
    pd_next = None
    for r in range(nsub + 2):
        pd_cur, pd_next = pd_next, None
        s_cur, s_next = s_next, None
        if r < nsub - 1:
            s_next = scores(r + 1)
        if s_cur is not None and r < nsub:
            pd_next = softmax(s_cur)
        if pd_cur is not None:
            weigh(r - 1, *pd_cur)


def _attn_a(pxl, pxc, sink, *, batch, seq, ctx_len, d, tq):
    kvh = d // HEAD_DIM // A_GROUP
    gw = A_GROUP * HEAD_DIM
    nq = seq // tq
    r = tq // WINDOW
    nblk = seq // WINDOW
    cols = _col_offsets(d)
    assert cols["q_a"] % gw == 0
    assert cols["z_a"] % gw == 0
    qa, za = cols["q_a"] // gw, cols["z_a"] // gw
    ka_l = ka_c = cols["k_a"] // HEAD_DIM
    va_l = va_c = cols["v_a"] // HEAD_DIM

    def prev(b, i, h):
        return (b * nblk + jnp.maximum(i * r - 1, 0), h)

    def nxt(b, i, h):
        return (b * nblk + jnp.minimum((i + 1) * r, nblk - 1), h)

    halo = (WINDOW, HEAD_DIM)
    return pl.pallas_call(
        functools.partial(_attn_a_kernel, seq=seq),
        out_shape=jax.ShapeDtypeStruct((batch * seq, d), BF16),
        grid=(batch, nq, kvh),
        in_specs=[
            pl.BlockSpec(memory_space=pltpu.SMEM),
            pl.BlockSpec((tq, gw), lambda b, i, h: (b * nq + i, qa + h)),
            pl.BlockSpec(halo, lambda b, i, h: prev(b, i, ka_l + h)),
            pl.BlockSpec((tq, HEAD_DIM), lambda b, i, h: (b * nq + i, ka_l + h)),
            pl.BlockSpec(halo, lambda b, i, h: nxt(b, i, ka_l + h)),
            pl.BlockSpec(halo, lambda b, i, h: prev(b, i, va_l + h)),
            pl.BlockSpec((tq, HEAD_DIM), lambda b, i, h: (b * nq + i, va_l + h)),
            pl.BlockSpec(halo, lambda b, i, h: nxt(b, i, va_l + h)),
            pl.BlockSpec((ctx_len, HEAD_DIM), lambda b, i, h: (b, ka_c + h)),
            pl.BlockSpec((ctx_len, HEAD_DIM), lambda b, i, h: (b, va_c + h)),
            pl.BlockSpec((tq, gw), lambda b, i, h: (b * nq + i, za + h)),
        ],
        out_specs=pl.BlockSpec((tq, gw), lambda b, i, h: (b * nq + i, h)),
        compiler_params=_cparams("parallel", "parallel", "arbitrary"),
        name="attn_a_latent",
    )(sink, pxl, pxl, pxl, pxl, pxl, pxl, pxl, pxc, pxc, pxl)


def _attn_a_ctx_kernel(sink_ref, q_ref, k_ref, v_ref, z_ref, o_ref):
    kvh = pl.program_id(1)
    rows = q_ref.shape[0]
    q = _stack_heads(q_ref[...])
    v = v_ref[...]
    sink = _sink_column(sink_ref, kvh, rows) * LOG2E
    s = _qkt(q, k_ref[...])
    m = jnp.maximum(jnp.max(s, axis=-1, keepdims=True), sink)
    p = jnp.exp2(s - m)
    denom = jnp.sum(p, axis=-1, keepdims=True) + jnp.exp2(sink - m)
    o = _unstack_heads(jnp.dot(p.astype(v.dtype), v, preferred_element_type=F32) / denom, A_GROUP)
    o_ref[...] = _gate_out(o, z_ref[...].astype(F32), o_ref.dtype)


def _attn_a_ctx(pxc, sink, *, batch, ctx_len, d):
    kvh = d // HEAD_DIM // A_GROUP
    gw = A_GROUP * HEAD_DIM
    cols = _col_offsets(d)
    qa, za = cols["q_a"] // gw, cols["z_a"] // gw
    ka_c, va_c = cols["k_a"] // HEAD_DIM, cols["v_a"] // HEAD_DIM
    return pl.pallas_call(
        _attn_a_ctx_kernel,
        out_shape=jax.ShapeDtypeStruct((batch * ctx_len, d), BF16),
        grid=(batch, kvh),
        in_specs=[pl.BlockSpec(memory_space=pltpu.SMEM),
                  pl.BlockSpec((ctx_len, gw), lambda b, h: (b, qa + h)),
                  pl.BlockSpec((ctx_len, HEAD_DIM), lambda b, h: (b, ka_c + h)),
                  pl.BlockSpec((ctx_len, HEAD_DIM), lambda b, h: (b, va_c + h)),
                  pl.BlockSpec((ctx_len, gw), lambda b, h: (b, za + h))],
        out_specs=pl.BlockSpec((ctx_len, gw), lambda b, h: (b, h)),
        compiler_params=_cparams("parallel", "arbitrary"),
        name="attn_a_context",
    )(sink, pxc, pxc, pxc, pxc)


def _row_sum_lanes(x):
    groups = [x[:, g * HEAD_DIM:(g + 1) * HEAD_DIM] for g in range(x.shape[1] // HEAD_DIM)]
    part = functools.reduce(jnp.add, groups)
    return jnp.broadcast_to(jnp.sum(part, axis=-1, keepdims=True), part.shape)


def _diff_finish(o0, l0, o1, l1, lamqk_ref, g_ref, lam_init):
    lq = lamqk_ref[...]
    lam = (jnp.exp(jnp.sum(lq[0:1] * lq[1:2], axis=-1, keepdims=True))
           - jnp.exp(jnp.sum(lq[2:3] * lq[3:4], axis=-1, keepdims=True)) + lam_init)
    wide = lambda t: jnp.concatenate([t] * (o0.shape[1] // HEAD_DIM), axis=1)
    o = o0 / wide(l0) - lam * (o1 / wide(l1))
    inv = lax.rsqrt(_row_sum_lanes(o * o) * (1.0 / o.shape[1]) + EPS)
    return o * wide(inv) * g_ref[...] * (1.0 - lam_init)


def _attn_b_ctx_kernel(q_ref, k_ref, v_ref, lamqk_ref, g_ref, z_ref, o_ref, *, lam_init):
    v = v_ref[...]
    parts = []
    for mp in range(2):
        cols = slice(mp * HEAD_DIM, (mp + 1) * HEAD_DIM)
        s = _qkt(q_ref[:, cols], k_ref[:, cols])
        p = jnp.exp2(s - jnp.max(s, axis=-1, keepdims=True))
        parts += [jnp.dot(p.astype(v.dtype), v, preferred_element_type=F32), _row_sum_lanes(p)]
    o = _diff_finish(*parts, lamqk_ref, g_ref, lam_init)
    o_ref[...] = _gate_out(o, z_ref[...].astype(F32), o_ref.dtype)


def _attn_b_ctx(pxc, lamqk, g_sub, *, lam_init, batch, ctx_len, d):
    hw = 2 * HEAD_DIM
    heads = d // hw
    cols = _col_offsets(d)
    qb, kb, vb, zb = cols["q_b"] // hw, cols["k_b"] // hw, cols["v_b"] // hw, cols["z_b"] // hw
    return pl.pallas_call(
        functools.partial(_attn_b_ctx_kernel, lam_init=lam_init),
        out_shape=jax.ShapeDtypeStruct((batch * ctx_len, d), BF16),
        grid=(batch, heads),
        in_specs=[pl.BlockSpec((ctx_len, hw), lambda b, h: (b, qb + h)),
                  pl.BlockSpec((ctx_len, hw), lambda b, h: (b, kb + h)),
                  pl.BlockSpec((ctx_len, hw), lambda b, h: (b, vb + h)),
                  pl.BlockSpec((4, HEAD_DIM), lambda b, h: (0, 0)),
                  pl.BlockSpec((1, hw), lambda b, h: (0, 0)),
                  pl.BlockSpec((ctx_len, hw), lambda b, h: (b, zb + h))],
        out_specs=pl.BlockSpec((ctx_len, hw), lambda b, h: (b, h)),
        compiler_params=_cparams("parallel", "arbitrary"),
        name="attn_b_context",
    )(pxc, pxc, pxc, lamqk, g_sub.reshape(1, hw), pxc)


def _attn_b_latent_kernel(q_ref, k_ref, v_ref, kc_ref, vc_ref, lamqk_ref, g_ref, z_ref, o_ref,
                          sa_scr, sb_scr, xa_scr, xb_scr, pa_scr, pb_scr, ala_scr, alb_scr,
                          m_scr, l_scr, acc_scr, *, lam_init, tk):
    nk = k_ref.shape[0] // tk
    tq = q_ref.shape[0]
    lanes = HEAD_DIM

    def lane_groups(x):
        return [x[:, g * lanes:(g + 1) * lanes] for g in range(x.shape[1] // lanes)]

    def group_max(s):
        return functools.reduce(jnp.maximum, lane_groups(s))

    def row_max(gmax):
        return jnp.broadcast_to(jnp.max(gmax, axis=-1, keepdims=True), (tq, lanes))

    def scores(c, buf):
        s_scr, x_scr = buf[0], buf[1]
        start = pl.multiple_of(c * tk, tk)
        for mp in range(2):
            cols = slice(mp * HEAD_DIM, (mp + 1) * HEAD_DIM)
            s = _qkt(q_ref[:, cols], k_ref[pl.ds(start, tk), cols])
            s_scr[mp] = s
            x_scr[mp] = group_max(s)

    def softmax(buf):
        s_scr, x_scr, p_scr, al_scr = buf
        for mp in range(2):
            s = s_scr[mp]
            m_old = m_scr[mp]
            m_new = jnp.maximum(m_old, row_max(x_scr[mp]))
            alpha = jnp.exp2(m_old - m_new)
            psum = None
            for g, sg in enumerate(lane_groups(s)):
                pg = jnp.exp2(sg - m_new)
                p_scr[mp, :, g * lanes:(g + 1) * lanes] = pg.astype(p_scr.dtype)
                psum = pg if psum is None else psum + pg
            l_scr[mp] = alpha * l_scr[mp] + psum
            al_scr[mp] = alpha
            m_scr[mp] = m_new

    def weigh(c, buf):
        p_scr, al_scr = buf[2], buf[3]
        v = v_ref[pl.ds(pl.multiple_of(c * tk, tk), tk), :]
        for mp in range(2):
            al = al_scr[mp]
            al = jnp.concatenate([al] * (acc_scr.shape[2] // lanes), axis=1)
            acc_scr[mp] = al * acc_scr[mp] + jnp.dot(p_scr[mp], v, preferred_element_type=F32)

    buf_a = (sa_scr, xa_scr, pa_scr, ala_scr)
    buf_b = (sb_scr, xb_scr, pb_scr, alb_scr)
    scores(0, buf_a)
    vc = vc_ref[...]
    for mp in range(2):
        cols = slice(mp * HEAD_DIM, (mp + 1) * HEAD_DIM)
        s = _qkt(q_ref[:, cols], kc_ref[:, cols])
        m0 = row_max(group_max(s))
        pgs = [jnp.exp2(sg - m0) for sg in lane_groups(s)]
        m_scr[mp] = m0
        l_scr[mp] = functools.reduce(jnp.add, pgs)
        acc_scr[mp] = jnp.dot(jnp.concatenate(pgs, axis=1).astype(vc.dtype), vc,
                              preferred_element_type=F32)
    softmax(buf_a)
    scores(1, buf_b)

    def pair(i, carry):
        weigh(2 * i, buf_a)
        softmax(buf_b)
        scores(2 * i + 2, buf_a)
        weigh(2 * i + 1, buf_b)
        softmax(buf_a)
        scores(2 * i + 3, buf_b)
        return carry

    lax.fori_loop(0, nk // 2 - 1, pair, 0)
    weigh(nk - 2, buf_a)
    softmax(buf_b)
    weigh(nk - 1, buf_b)

    o = _diff_finish(acc_scr[0], _row_sum_lanes(l_scr[0]), acc_scr[1], _row_sum_lanes(l_scr[1]),
                     lamqk_ref, g_ref, lam_init)
    o_ref[...] = _gate_out(o, z_ref[...].astype(F32), o_ref.dtype)


def _attn_b_latent(pxl, pxc, lamqk, g_sub, *, lam_init, batch, seq, ctx_len, d, tq, tk):
    hw = 2 * HEAD_DIM
    heads = d // hw
    nq = seq // tq
    assert (seq // tk) % 2 == 0 and seq // tk >= 2
    cols = _col_offsets(d)
    qb, zb = cols["q_b"] // hw, cols["z_b"] // hw
    kb_l = kb_c = cols["k_b"] // hw
    vb_l = vb_c = cols["v_b"] // hw
    return pl.pallas_call(
        functools.partial(_attn_b_latent_kernel, lam_init=lam_init, tk=tk),
        out_shape=jax.ShapeDtypeStruct((batch * seq, d), BF16),
        grid=(batch, heads, nq),
        in_specs=[pl.BlockSpec((tq, hw), lambda b, h, i: (b * nq + i, qb + h)),
                  pl.BlockSpec((seq, hw), lambda b, h, i: (b, kb_l + h)),
                  pl.BlockSpec((seq, hw), lambda b, h, i: (b, vb_l + h)),
                  pl.BlockSpec((ctx_len, hw), lambda b, h, i: (b, kb_c + h)),
                  pl.BlockSpec((ctx_len, hw), lambda b, h, i: (b, vb_c + h)),
                  pl.BlockSpec((4, HEAD_DIM), lambda b, h, i: (0, 0)),
                  pl.BlockSpec((1, hw), lambda b, h, i: (0, 0)),
                  pl.BlockSpec((tq, hw), lambda b, h, i: (b * nq + i, zb + h))],
        out_specs=pl.BlockSpec((tq, hw), lambda b, h, i: (b * nq + i, h)),
        scratch_shapes=[pltpu.VMEM((2, tq, tk), F32), pltpu.VMEM((2, tq, tk), F32),
                        pltpu.VMEM((2, tq, HEAD_DIM), F32), pltpu.VMEM((2, tq, HEAD_DIM), F32),
                        pltpu.VMEM((2, tq, tk), BF16), pltpu.VMEM((2, tq, tk), BF16),
                        pltpu.VMEM((2, tq, HEAD_DIM), F32), pltpu.VMEM((2, tq, HEAD_DIM), F32),
                        pltpu.VMEM((2, tq, HEAD_DIM), F32), pltpu.VMEM((2, tq, HEAD_DIM), F32),
                        pltpu.VMEM((2, tq, hw), F32)],
        compiler_params=_cparams("parallel", "parallel", "arbitrary"),
        name="attn_b_latent",
    )(pxl, pxl, pxl, pxc, pxc, lamqk, g_sub.reshape(1, hw), pxl)


def _merge1_kernel(a_ref, b_ref, ga_ref, gb_ref, wa_ref, wb_ref, u_ref):
    ya = jnp.dot(a_ref[...], wa_ref[...], preferred_element_type=F32)
    yb = jnp.dot(b_ref[...], wb_ref[...], preferred_element_type=F32)
    u = _sigmoid(ga_ref[...].astype(F32)) * ya + _sigmoid(gb_ref[...].astype(F32)) * yb
    u_ref[...] = u.astype(u_ref.dtype)


def _merge1(a, b, px, wpa, wpb, *, d, tm, tn):
    t = a.shape[0]
    cols = _col_offsets(d)
    assert cols["g_a"] % tn == 0 and cols["g_b"] % tn == 0
    ga, gb = cols["g_a"] // tn, cols["g_b"] // tn
    return pl.pallas_call(
        _merge1_kernel,
        out_shape=jax.ShapeDtypeStruct((t, d), BF16),
        grid=(t // tm, d // tn),
        in_specs=[pl.BlockSpec((tm, d), lambda i, j: (i, 0)),
                  pl.BlockSpec((tm, d), lambda i, j: (i, 0)),
                  pl.BlockSpec((tm, tn), lambda i, j: (i, ga + j)),
                  pl.BlockSpec((tm, tn), lambda i, j: (i, gb + j)),
                  pl.BlockSpec((d, tn), lambda i, j: (0, j)),
                  pl.BlockSpec((d, tn), lambda i, j: (0, j))],
        out_specs=pl.BlockSpec((tm, tn), lambda i, j: (i, j)),
        compiler_params=_cparams("parallel", "arbitrary"),
        name="merge_gate",
    )(a, b, px, px, wpa, wpb)


def _merge2_kernel(u_ref, wo_ref, x_ref, g_ref, mod_ref, o_ref, *, d):
    y = jnp.dot(u_ref[...], wo_ref[...], preferred_element_type=F32)
    n = y * lax.rsqrt(jnp.mean(y * y, axis=-1, keepdims=True) + EPS) * g_ref[...]
    o_ref[...] = x_ref[...] + mod_ref[:, 2 * d:3 * d] * n


def _merge2(u, wo, x2, g_post, mod3, row_of_tile, tm):
    t, d = x2.shape
    return pl.pallas_call(
        functools.partial(_merge2_kernel, d=d),
        out_shape=jax.ShapeDtypeStruct((t, d), F32),
        grid=(t // tm,),
        in_specs=[pl.BlockSpec((tm, d), lambda i: (i, 0)),
                  pl.BlockSpec((d, d), lambda i: (0, 0)),
                  pl.BlockSpec((tm, d), lambda i: (i, 0)),
                  pl.BlockSpec((1, d), lambda i: (0, 0)),
                  pl.BlockSpec((None, 1, 3 * d), lambda i: (row_of_tile(i), 0, 0))],
        out_specs=pl.BlockSpec((tm, d), lambda i: (i, 0)),
        compiler_params=_cparams("parallel"),
        name="merge_out",
    )(u, wo, x2, g_post.reshape(1, d), mod3)


def _rope_tables(seq):
    rows = seq // GRID_W
    r = jnp.repeat(jnp.arange(rows, dtype=F32), GRID_W)
    col = jnp.tile(jnp.arange(GRID_W, dtype=F32), rows)
    n_freq = HEAD_DIM // 4
    inv = ROPE_THETA ** (-jnp.arange(n_freq, dtype=F32) / n_freq)
    ang = jnp.concatenate([r[:, None] * inv, col[:, None] * inv], axis=-1)
    cos, sin = jnp.cos(ang), jnp.sin(ang)
    return jnp.concatenate([cos, cos], axis=-1), jnp.concatenate([-sin, sin], axis=-1)


def _tile_types(d, tn, rope):
    cols = _col_offsets(d)
    r = 1 if rope else 0
    kinds = {"k_a": r, "k_b": r, "q_a": r | 2 | 4, "q_b": r | 2 | 4}
    out = [0] * (cols["in_cols"] // tn)
    for name, ty in kinds.items():
        width = d // A_GROUP if name == "k_a" else d
        assert cols[name] % tn == 0 and width % tn == 0
        for t in range(cols[name] // tn, (cols[name] + width) // tn):
            out[t] = ty
    return jnp.asarray(out, jnp.int32)


def kernel(x, c, ctx, c_ctx, w_ada, b_ada, g_pre, g_post, w_in, sink, lam_qk, g_subln,
           w_proj_a, w_proj_b, w_out):
    batch, seq, d = x.shape
    ctx_len = ctx.shape[1]
    depth = w_in.shape[0]
    t_lat, t_ctx = batch * seq, batch * ctx_len
    cols = _col_offsets(d)
    kv_cols, in_cols = cols["kv_cols"], cols["in_cols"]

    tn_in = min(512, d // A_GROUP)
    tm_lat = _pick(seq, (1024, 512, 256))
    tm_in = _pick(seq, (2048, 1024, 512, 256))
    tm_ctx = _pick(t_ctx, (512, 256))
    tq_a = _pick(seq, (512, 256, 128))
    tq_b = _pick(seq, (1024, 512, 256))
    tk_b = _pick(seq // 2, (512, 256))
    tn_m = _pick(d // 2, (1024, 512))
    tm_m = _pick(seq, (512, 256))

    cosf, sinf = _rope_tables(seq)
    zeros_tab = jnp.zeros((tm_ctx, HEAD_DIM), F32)
    types_lat = _tile_types(d, tn_in, rope=True)
    types_ctx = _tile_types(d, tn_in, rope=False)
    cvec8 = jnp.concatenate([c, c_ctx[None, :], jnp.zeros((8 - batch - 1, d), F32)], axis=0)

    x2 = x.reshape(t_lat, d)
    c2 = ctx.reshape(t_ctx, d)
    lat_row = lambda tm: (lambda i: i // (seq // tm))
    ctx_row = lambda i: batch

    for l in range(depth):
        last = l == depth - 1
        lam_init = 0.8 - 0.6 * math.exp(-0.3 * l)
        w_l = w_in[l].astype(BF16)
        wpa, wpb, wo = w_proj_a[l].astype(BF16), w_proj_b[l].astype(BF16), w_out[l].astype(BF16)
        mod3 = _ada(cvec8, w_ada, b_ada, l).reshape(8, 1, 3 * d)

        hx = _prenorm(x2, g_pre[l], mod3, lat_row(tm_lat), tm_lat)
        hc = _prenorm(c2, g_pre[l], mod3, ctx_row, tm_ctx)
        pxl = _inproj(hx, w_l, types_lat, cosf, sinf, ncols=in_cols,
                      tm=tm_in, tn=tn_in, pos_tiles=seq // tm_in)
        pxc = _inproj(hc, w_l, types_ctx, zeros_tab, zeros_tab, ncols=kv_cols if last else in_cols,
                      tm=tm_ctx, tn=tn_in, pos_tiles=1)

        o_a = _attn_a(pxl, pxc, sink[l], batch=batch, seq=seq, ctx_len=ctx_len, d=d, tq=tq_a)
        o_b = _attn_b_latent(pxl, pxc, lam_qk[l], g_subln[l], lam_init=lam_init, batch=batch,
                             seq=seq, ctx_len=ctx_len, d=d, tq=tq_b, tk=tk_b)
        u = _merge1(o_a, o_b, pxl, wpa, wpb, d=d, tm=tm_m, tn=tn_m)

        if not last:
            oc_a = _attn_a_ctx(pxc, sink[l], batch=batch, ctx_len=ctx_len, d=d)
            oc_b = _attn_b_ctx(pxc, lam_qk[l], g_subln[l], lam_init=lam_init, batch=batch,
                               ctx_len=ctx_len, d=d)
            uc = _merge1(oc_a, oc_b, pxc, wpa, wpb, d=d, tm=tm_ctx, tn=tn_m)
            c2 = _merge2(uc, wo, c2, g_post[l], mod3, ctx_row, tm_ctx)

        x2 = _merge2(u, wo, x2, g_post[l], mod3, lat_row(tm_m), tm_m)

    return x2.reshape(batch, seq, d)
```

```python
import functools
import math

import jax
import jax.numpy as jnp
from jax import lax
from jax.experimental import pallas as pl
from jax.experimental.pallas import tpu as pltpu

HEAD_DIM = 128
GRID_W = 64
WINDOW = 128
A_GROUP = 4
ROPE_THETA = 10000.0
EPS = 1e-6
NEG = -1e30
Q_SCALE = HEAD_DIM ** -0.5
LOG2E = math.log2(math.e)
VMEM_LIMIT_BYTES = 56 * 1024 * 1024

F32 = jnp.float32
BF16 = jnp.bfloat16


def _cparams(*semantics):
    return pltpu.CompilerParams(dimension_semantics=semantics, vmem_limit_bytes=VMEM_LIMIT_BYTES)


def _pick(n, candidates):
    for c in candidates:
        if n % c == 0:
            return c
    raise ValueError(f"no tile in {candidates} divides {n}")


def _sigmoid(v):
    return 0.5 * jnp.tanh(0.5 * v) + 0.5


def _silu(v):
    h = 0.5 * v
    return h * jnp.tanh(h) + h


def _gate_out(o, z, dtype):
    return (o * _silu(z)).astype(dtype)


def _col_offsets(d):
    kv = d // A_GROUP
    names = ("k_a", "v_a", "k_b", "v_b", "q_a", "z_a", "q_b", "z_b", "g_a", "g_b")
    widths = (kv, kv, d, d, d, d, d, d, d, d)
    off, out = 0, {}
    for n, w in zip(names, widths):
        out[n] = off
        off += w
    out["kv_cols"] = out["q_a"]
    out["in_cols"] = off
    return out


def _ada_kernel(c_ref, w_ref, b_ref, o_ref):
    a = _silu(c_ref[...])
    o_ref[...] = jnp.dot(a, w_ref[...], preferred_element_type=F32,
                         precision=lax.Precision.HIGHEST) + b_ref[...]


def _ada(cvec8, w_all, b_all, layer):
    depth, d, n = w_all.shape
    tn = _pick(n, (1024, 768, 512, 384, 256, 128))
    return pl.pallas_call(
        _ada_kernel,
        out_shape=jax.ShapeDtypeStruct((8, n), F32),
        grid=(n // tn,),
        in_specs=[pl.BlockSpec((8, d), lambda j: (0, 0)),
                  pl.BlockSpec((None, d, tn), lambda j: (layer, 0, j)),
                  pl.BlockSpec((None, 1, tn), lambda j: (layer, 0, j))],
        out_specs=pl.BlockSpec((8, tn), lambda j: (0, j)),
        compiler_params=_cparams("arbitrary"),
        name="ada_ln",
    )(cvec8, w_all, b_all.reshape(depth, 1, n))


def _prenorm_kernel(x_ref, g_ref, mod_ref, o_ref, *, d):
    x = x_ref[...]
    y = x * lax.rsqrt(jnp.mean(x * x, axis=-1, keepdims=True) + EPS)
    y = y * g_ref[...]
    shift = mod_ref[:, 0:d]
    scale = mod_ref[:, d:2 * d]
    o_ref[...] = (y * (1.0 + scale) + shift).astype(o_ref.dtype)


def _prenorm(x2, g, mod3, row_of_tile, tm):
    t, d = x2.shape
    return pl.pallas_call(
        functools.partial(_prenorm_kernel, d=d),
        out_shape=jax.ShapeDtypeStruct((t, d), BF16),
        grid=(t // tm,),
        in_specs=[pl.BlockSpec((tm, d), lambda i: (i, 0)),
                  pl.BlockSpec((1, d), lambda i: (0, 0)),
                  pl.BlockSpec((None, 1, 3 * d), lambda i: (row_of_tile(i), 0, 0))],
        out_specs=pl.BlockSpec((tm, d), lambda i: (i, 0)),
        compiler_params=_cparams("parallel"),
        name="prenorm",
    )(x2, g.reshape(1, d), mod3)


def _inproj_kernel(type_ref, h_ref, w_ref, cos_ref, sin_ref, o_ref):
    t = type_ref[pl.program_id(1)]
    heads = o_ref.shape[1] // HEAD_DIM

    def matmul():
        return jnp.dot(h_ref[...], w_ref[...].astype(h_ref.dtype), preferred_element_type=F32)

    @pl.when(t == 0)
    def _():
        o_ref[...] = matmul().astype(o_ref.dtype)

    @pl.when(t != 0)
    def _():
        acc = matmul()
        rope = (t & 1).astype(F32)
        scale = (jnp.where((t & 2) != 0, Q_SCALE, 1.0) * jnp.where((t & 4) != 0, LOG2E, 1.0)).astype(F32)
        ca = (rope * cos_ref[...] + (1.0 - rope)) * scale
        sa = (rope * sin_ref[...]) * scale
        for hh in range(heads):
            a = acc[:, hh * HEAD_DIM:(hh + 1) * HEAD_DIM]
            r = a * ca + pltpu.roll(a, HEAD_DIM // 2, 1) * sa
            o_ref[:, hh * HEAD_DIM:(hh + 1) * HEAD_DIM] = r.astype(o_ref.dtype)


def _inproj(h, w_all, layer, types, cosf, sinf, *, ncols, tm, tn, pos_tiles):
    t, d = h.shape
    grid_spec = pltpu.PrefetchScalarGridSpec(
        num_scalar_prefetch=1,
        grid=(t // tm, ncols // tn),
        in_specs=[pl.BlockSpec((tm, d), lambda i, j, ty: (i, 0)),
                  pl.BlockSpec((None, d, tn), lambda i, j, ty: (layer, 0, j)),
                  pl.BlockSpec((tm, HEAD_DIM), lambda i, j, ty: (i % pos_tiles, 0)),
                  pl.BlockSpec((tm, HEAD_DIM), lambda i, j, ty: (i % pos_tiles, 0))],
        out_specs=pl.BlockSpec((tm, tn), lambda i, j, ty: (i, j)),
    )
    return pl.pallas_call(
        _inproj_kernel,
        out_shape=jax.ShapeDtypeStruct((t, ncols), BF16),
        grid_spec=grid_spec,
        compiler_params=_cparams("parallel", "arbitrary"),
        name="in_proj",
    )(types, h, w_all, cosf, sinf)


def _stack_heads(q):
    g = q.shape[1] // HEAD_DIM
    return jnp.concatenate([q[:, i * HEAD_DIM:(i + 1) * HEAD_DIM] for i in range(g)], axis=0)


def _unstack_heads(o, g):
    rows = o.shape[0] // g
    return jnp.concatenate([o[i * rows:(i + 1) * rows, :] for i in range(g)], axis=1)


def _sink_column(sink_ref, kvh, rows):
    return jnp.concatenate(
        [jnp.full((rows, 1), sink_ref[kvh * A_GROUP + g], F32) for g in range(A_GROUP)], axis=0)


def _qkt(q, k):
    return lax.dot_general(q, k, (((1,), (1,)), ((), ())), preferred_element_type=F32)


def _attn_a_kernel(sink_ref, q_ref, kp_ref, km_ref, kn_ref, vp_ref, vm_ref, vn_ref, kc_ref, vc_ref,
                   z_ref, o_ref, *, seq):
    i = pl.program_id(1)
    kvh = pl.program_id(2)
    tq = q_ref.shape[0]
    k_win = jnp.concatenate([kp_ref[...], km_ref[...], kn_ref[...]], axis=0)
    v_win = jnp.concatenate([vp_ref[...], vm_ref[...], vn_ref[...]], axis=0)
    kc = kc_ref[...]
    vc = vc_ref[...]
    rows = A_GROUP * WINDOW
    nloc = 3 * WINDOW
    nkeys = nloc + kc.shape[0]
    sink = _sink_column(sink_ref, kvh, WINDOW) * LOG2E
    qi = lax.broadcasted_iota(jnp.int32, (rows, nkeys), 0) & (WINDOW - 1)
    kj = lax.broadcasted_iota(jnp.int32, (rows, nkeys), 1)
    band = jnp.abs(qi + WINDOW - kj) <= WINDOW
    is_ctx = kj >= nloc
    nsub = tq // WINDOW

    def scores(r):
        q = _stack_heads(q_ref[r * WINDOW:(r + 1) * WINDOW, :])
        k = jnp.concatenate([k_win[r * WINDOW:(r + 3) * WINDOW, :], kc], axis=0)
        base = i * tq + (r - 1) * WINDOW
        valid = is_ctx | (band & (kj >= -base) & (kj < seq - base))
        return jnp.where(valid, _qkt(q, k), NEG)

    def lane_bcast(col):
        return jnp.broadcast_to(col, (rows, HEAD_DIM))

    sink_b = lane_bcast(sink)

    def softmax(s):
        groups = [s[:, g * HEAD_DIM:(g + 1) * HEAD_DIM] for g in range(nkeys // HEAD_DIM)]
        gmax = functools.reduce(jnp.maximum, groups)
        m = jnp.maximum(lane_bcast(jnp.max(gmax, axis=-1, keepdims=True)), sink_b)
        ps = [jnp.exp2(sg - m) for sg in groups]
        psum = functools.reduce(jnp.add, ps)
        denom = lane_bcast(jnp.sum(psum, axis=-1, keepdims=True)) + jnp.exp2(sink_b - m)
        return jnp.concatenate([pg.astype(vc.dtype) for pg in ps], axis=1), denom

    def weigh(r, p, denom):
        v = jnp.concatenate([v_win[r * WINDOW:(r + 3) * WINDOW, :], vc], axis=0)
        o = _unstack_heads(jnp.dot(p, v, preferred_element_type=F32) / denom, A_GROUP)
        z = z_ref[r * WINDOW:(r + 1) * WINDOW, :].astype(F32)
        o_ref[r * WINDOW:(r + 1) * WINDOW, :] = _gate_out(o, z, o_ref.dtype)

    s_next = scores(0)
    pd_next = None
    for r in range(nsub + 2):
        pd_cur, pd_next = pd_next, None
        s_cur, s_next = s_next, None
        if r < nsub - 1:
            s_next = scores(r + 1)
        if s_cur is not None and r < nsub:
            pd_next = softmax(s_cur)
        if pd_cur is not None:
            weigh(r - 1, *pd_cur)


def _attn_a(pxl, pxc, sink, *, batch, seq, ctx_len, d, tq):
    kvh = d // HEAD_DIM // A_GROUP
    gw = A_GROUP * HEAD_DIM
    nq = seq // tq
    r = tq // WINDOW
    nblk = seq // WINDOW
    cols = _col_offsets(d)
    assert cols["q_a"] % gw == 0
    assert cols["z_a"] % gw == 0
    qa, za = cols["q_a"] // gw, cols["z_a"] // gw
    ka_l = ka_c = cols["k_a"] // HEAD_DIM
    va_l = va_c = cols["v_a"] // HEAD_DIM

    def prev(b, i, h):
        return (b * nblk + jnp.maximum(i * r - 1, 0), h)

    def nxt(b, i, h):
        return (b * nblk + jnp.minimum((i + 1) * r, nblk - 1), h)

    halo = (WINDOW, HEAD_DIM)
    return pl.pallas_call(
        functools.partial(_attn_a_kernel, seq=seq),
        out_shape=jax.ShapeDtypeStruct((batch * seq, d), BF16),
        grid=(batch, nq, kvh),
        in_specs=[
            pl.BlockSpec(memory_space=pltpu.SMEM),
            pl.BlockSpec((tq, gw), lambda b, i, h: (b * nq + i, qa + h)),
            pl.BlockSpec(halo, lambda b, i, h: prev(b, i, ka_l + h)),
            pl.BlockSpec((tq, HEAD_DIM), lambda b, i, h: (b * nq + i, ka_l + h)),
            pl.BlockSpec(halo, lambda b, i, h: nxt(b, i, ka_l + h)),
            pl.BlockSpec(halo, lambda b, i, h: prev(b, i, va_l + h)),
            pl.BlockSpec((tq, HEAD_DIM), lambda b, i, h: (b * nq + i, va_l + h)),
            pl.BlockSpec(halo, lambda b, i, h: nxt(b, i, va_l + h)),
            pl.BlockSpec((ctx_len, HEAD_DIM), lambda b, i, h: (b, ka_c + h)),
            pl.BlockSpec((ctx_len, HEAD_DIM), lambda b, i, h: (b, va_c + h)),
            pl.BlockSpec((tq, gw), lambda b, i, h: (b * nq + i, za + h)),
        ],
        out_specs=pl.BlockSpec((tq, gw), lambda b, i, h: (b * nq + i, h)),
        compiler_params=_cparams("parallel", "parallel", "arbitrary"),
        name="attn_a_latent",
    )(sink, pxl, pxl, pxl, pxl, pxl, pxl, pxl, pxc, pxc, pxl)


def _attn_a_ctx_kernel(sink_ref, q_ref, k_ref, v_ref, z_ref, o_ref):
    kvh = pl.program_id(1)
    rows = q_ref.shape[0]
    q = _stack_heads(q_ref[...])
    v = v_ref[...]
    sink = _sink_column(sink_ref, kvh, rows) * LOG2E
    s = _qkt(q, k_ref[...])
    m = jnp.maximum(jnp.max(s, axis=-1, keepdims=True), sink)
    p = jnp.exp2(s - m)
    denom = jnp.sum(p, axis=-1, keepdims=True) + jnp.exp2(sink - m)
    o = _unstack_heads(jnp.dot(p.astype(v.dtype), v, preferred_element_type=F32) / denom, A_GROUP)
    o_ref[...] = _gate_out(o, z_ref[...].astype(F32), o_ref.dtype)


def _attn_a_ctx(pxc, sink, *, batch, ctx_len, d):
    kvh = d // HEAD_DIM // A_GROUP
    gw = A_GROUP * HEAD_DIM
    cols = _col_offsets(d)
    qa, za = cols["q_a"] // gw, cols["z_a"] // gw
    ka_c, va_c = cols["k_a"] // HEAD_DIM, cols["v_a"] // HEAD_DIM
    return pl.pallas_call(
        _attn_a_ctx_kernel,
        out_shape=jax.ShapeDtypeStruct((batch * ctx_len, d), BF16),
        grid=(batch, kvh),
        in_specs=[pl.BlockSpec(memory_space=pltpu.SMEM),
                  pl.BlockSpec((ctx_len, gw), lambda b, h: (b, qa + h)),
                  pl.BlockSpec((ctx_len, HEAD_DIM), lambda b, h: (b, ka_c + h)),
                  pl.BlockSpec((ctx_len, HEAD_DIM), lambda b, h: (b, va_c + h)),
                  pl.BlockSpec((ctx_len, gw), lambda b, h: (b, za + h))],
        out_specs=pl.BlockSpec((ctx_len, gw), lambda b, h: (b, h)),
        compiler_params=_cparams("parallel", "arbitrary"),
        name="attn_a_context",
    )(sink, pxc, pxc, pxc, pxc)


def _row_sum_lanes(x):
    groups = [x[:, g * HEAD_DIM:(g + 1) * HEAD_DIM] for g in range(x.shape[1] // HEAD_DIM)]
    part = functools.reduce(jnp.add, groups)
    return jnp.broadcast_to(jnp.sum(part, axis=-1, keepdims=True), part.shape)


def _diff_finish(o0, l0, o1, l1, lamqk_ref, g_ref, lam_init):
    lq = lamqk_ref[...]
    lam = (jnp.exp(jnp.sum(lq[0:1] * lq[1:2], axis=-1, keepdims=True))
           - jnp.exp(jnp.sum(lq[2:3] * lq[3:4], axis=-1, keepdims=True)) + lam_init)
    wide = lambda t: jnp.concatenate([t] * (o0.shape[1] // HEAD_DIM), axis=1)
    o = o0 / wide(l0) - lam * (o1 / wide(l1))
    inv = lax.rsqrt(_row_sum_lanes(o * o) * (1.0 / o.shape[1]) + EPS)
    return o * wide(inv) * g_ref[...] * (1.0 - lam_init)


def _attn_b_ctx_kernel(q_ref, k_ref, v_ref, lamqk_ref, g_ref, z_ref, o_ref, *, lam_init):
    v = v_ref[...]
    parts = []
    for mp in range(2):
        cols = slice(mp * HEAD_DIM, (mp + 1) * HEAD_DIM)
        s = _qkt(q_ref[:, cols], k_ref[:, cols])
        p = jnp.exp2(s - jnp.max(s, axis=-1, keepdims=True))
        parts += [jnp.dot(p.astype(v.dtype), v, preferred_element_type=F32), _row_sum_lanes(p)]
    o = _diff_finish(*parts, lamqk_ref, g_ref, lam_init)
    o_ref[...] = _gate_out(o, z_ref[...].astype(F32), o_ref.dtype)


def _attn_b_ctx(pxc, lamqk, g_sub, *, lam_init, batch, ctx_len, d):
    hw = 2 * HEAD_DIM
    heads = d // hw
    cols = _col_offsets(d)
    qb, kb, vb, zb = cols["q_b"] // hw, cols["k_b"] // hw, cols["v_b"] // hw, cols["z_b"] // hw
    return pl.pallas_call(
        functools.partial(_attn_b_ctx_kernel, lam_init=lam_init),
        out_shape=jax.ShapeDtypeStruct((batch * ctx_len, d), BF16),
        grid=(batch, heads),
        in_specs=[pl.BlockSpec((ctx_len, hw), lambda b, h: (b, qb + h)),
                  pl.BlockSpec((ctx_len, hw), lambda b, h: (b, kb + h)),
                  pl.BlockSpec((ctx_len, hw), lambda b, h: (b, vb + h)),
                  pl.BlockSpec((4, HEAD_DIM), lambda b, h: (0, 0)),
                  pl.BlockSpec((1, hw), lambda b, h: (0, 0)),
                  pl.BlockSpec((ctx_len, hw), lambda b, h: (b, zb + h))],
        out_specs=pl.BlockSpec((ctx_len, hw), lambda b, h: (b, h)),
        compiler_params=_cparams("parallel", "arbitrary"),
        name="attn_b_context",
    )(pxc, pxc, pxc, lamqk, g_sub.reshape(1, hw), pxc)


def _attn_b_latent_kernel(q_ref, k_ref, v_ref, kc_ref, vc_ref, lamqk_ref, g_ref, z_ref, o_ref,
                          sa_scr, sb_scr, xa_scr, xb_scr, pa_scr, pb_scr, ala_scr, alb_scr,
                          m_scr, l_scr, acc_scr, *, lam_init, tk):
    nk = k_ref.shape[0] // tk
    tq = q_ref.shape[0]
    lanes = HEAD_DIM

    def lane_groups(x):
        return [x[:, g * lanes:(g + 1) * lanes] for g in range(x.shape[1] // lanes)]

    def group_max(s):
        return functools.reduce(jnp.maximum, lane_groups(s))

    def row_max(gmax):
        return jnp.broadcast_to(jnp.max(gmax, axis=-1, keepdims=True), (tq, lanes))

    def scores(c, buf):
        s_scr, x_scr = buf[0], buf[1]
        start = pl.multiple_of(c * tk, tk)
        for mp in range(2):
            cols = slice(mp * HEAD_DIM, (mp + 1) * HEAD_DIM)
            s = _qkt(q_ref[:, cols], k_ref[pl.ds(start, tk), cols])
            s_scr[mp] = s
            x_scr[mp] = group_max(s)

    def softmax(buf):
        s_scr, x_scr, p_scr, al_scr = buf
        for mp in range(2):
            s = s_scr[mp]
            m_old = m_scr[mp]
            m_new = jnp.maximum(m_old, row_max(x_scr[mp]))
            alpha = jnp.exp2(m_old - m_new)
            psum = None
            for g, sg in enumerate(lane_groups(s)):
                pg = jnp.exp2(sg - m_new)
                p_scr[mp, :, g * lanes:(g + 1) * lanes] = pg.astype(p_scr.dtype)
                psum = pg if psum is None else psum + pg
            l_scr[mp] = alpha * l_scr[mp] + psum
            al_scr[mp] = alpha
            m_scr[mp] = m_new

    def weigh(c, buf):
        p_scr, al_scr = buf[2], buf[3]
        v = v_ref[pl.ds(pl.multiple_of(c * tk, tk), tk), :]
        for mp in range(2):
            al = al_scr[mp]
            al = jnp.concatenate([al] * (acc_scr.shape[2] // lanes), axis=1)
            acc_scr[mp] = al * acc_scr[mp] + jnp.dot(p_scr[mp], v, preferred_element_type=F32)

    buf_a = (sa_scr, xa_scr, pa_scr, ala_scr)
    buf_b = (sb_scr, xb_scr, pb_scr, alb_scr)
    scores(0, buf_a)
    vc = vc_ref[...]
    for mp in range(2):
        cols = slice(mp * HEAD_DIM, (mp + 1) * HEAD_DIM)
        s = _qkt(q_ref[:, cols], kc_ref[:, cols])
        m0 = row_max(group_max(s))
        pgs = [jnp.exp2(sg - m0) for sg in lane_groups(s)]
        m_scr[mp] = m0
        l_scr[mp] = functools.reduce(jnp.add, pgs)
        acc_scr[mp] = jnp.dot(jnp.concatenate(pgs, axis=1).astype(vc.dtype), vc,
                              preferred_element_type=F32)
    softmax(buf_a)
    scores(1, buf_b)

    def steps(c0, n):
        for t in range(n):
            cur, nxt = (buf_a, buf_b) if t % 2 == 0 else (buf_b, buf_a)
            weigh(c0 + t, cur)
            softmax(nxt)
            scores(c0 + t + 2, cur)

    unroll = 4
    trips = (nk - 2) // unroll

    def body(i, carry):
        steps(i * unroll, unroll)
        return carry

    lax.fori_loop(0, trips, body, 0)
    steps(trips * unroll, nk - 2 - trips * unroll)
    weigh(nk - 2, buf_a)
    softmax(buf_b)
    weigh(nk - 1, buf_b)

    o = _diff_finish(acc_scr[0], _row_sum_lanes(l_scr[0]), acc_scr[1], _row_sum_lanes(l_scr[1]),
                     lamqk_ref, g_ref, lam_init)
    o_ref[...] = _gate_out(o, z_ref[...].astype(F32), o_ref.dtype)


def _attn_b_latent(pxl, pxc, lamqk, g_sub, *, lam_init, batch, seq, ctx_len, d, tq, tk):
    hw = 2 * HEAD_DIM
    heads = d // hw
    nq = seq // tq
    assert (seq // tk) % 2 == 0 and seq // tk >= 2
    cols = _col_offsets(d)
    qb, zb = cols["q_b"] // hw, cols["z_b"] // hw
    kb_l = kb_c = cols["k_b"] // hw
    vb_l = vb_c = cols["v_b"] // hw
    return pl.pallas_call(
        functools.partial(_attn_b_latent_kernel, lam_init=lam_init, tk=tk),
        out_shape=jax.ShapeDtypeStruct((batch * seq, d), BF16),
        grid=(batch, heads, nq),
        in_specs=[pl.BlockSpec((tq, hw), lambda b, h, i: (b * nq + i, qb + h)),
                  pl.BlockSpec((seq, hw), lambda b, h, i: (b, kb_l + h)),
                  pl.BlockSpec((seq, hw), lambda b, h, i: (b, vb_l + h)),
                  pl.BlockSpec((ctx_len, hw), lambda b, h, i: (b, kb_c + h)),
                  pl.BlockSpec((ctx_len, hw), lambda b, h, i: (b, vb_c + h)),
                  pl.BlockSpec((4, HEAD_DIM), lambda b, h, i: (0, 0)),
                  pl.BlockSpec((1, hw), lambda b, h, i: (0, 0)),
                  pl.BlockSpec((tq, hw), lambda b, h, i: (b * nq + i, zb + h))],
        out_specs=pl.BlockSpec((tq, hw), lambda b, h, i: (b * nq + i, h)),
        scratch_shapes=[pltpu.VMEM((2, tq, tk), F32), pltpu.VMEM((2, tq, tk), F32),
                        pltpu.VMEM((2, tq, HEAD_DIM), F32), pltpu.VMEM((2, tq, HEAD_DIM), F32),
                        pltpu.VMEM((2, tq, tk), BF16), pltpu.VMEM((2, tq, tk), BF16),
                        pltpu.VMEM((2, tq, HEAD_DIM), F32), pltpu.VMEM((2, tq, HEAD_DIM), F32),
                        pltpu.VMEM((2, tq, HEAD_DIM), F32), pltpu.VMEM((2, tq, HEAD_DIM), F32),
                        pltpu.VMEM((2, tq, hw), F32)],
        compiler_params=_cparams("parallel", "parallel", "arbitrary"),
        name="attn_b_latent",
    )(pxl, pxl, pxl, pxc, pxc, lamqk, g_sub.reshape(1, hw), pxl)


def _merge1_kernel(a_ref, b_ref, ga_ref, gb_ref, wa_ref, wb_ref, u_ref):
    ya = jnp.dot(a_ref[...], wa_ref[...], preferred_element_type=F32)
    yb = jnp.dot(b_ref[...], wb_ref[...], preferred_element_type=F32)
    u = _sigmoid(ga_ref[...].astype(F32)) * ya + _sigmoid(gb_ref[...].astype(F32)) * yb
    u_ref[...] = u.astype(u_ref.dtype)


def _merge1(a, b, px, wpa, wpb, *, d, tm, tn):
    t = a.shape[0]
    cols = _col_offsets(d)
    assert cols["g_a"] % tn == 0 and cols["g_b"] % tn == 0
    ga, gb = cols["g_a"] // tn, cols["g_b"] // tn
    return pl.pallas_call(
        _merge1_kernel,
        out_shape=jax.ShapeDtypeStruct((t, d), BF16),
        grid=(t // tm, d // tn),
        in_specs=[pl.BlockSpec((tm, d), lambda i, j: (i, 0)),
                  pl.BlockSpec((tm, d), lambda i, j: (i, 0)),
                  pl.BlockSpec((tm, tn), lambda i, j: (i, ga + j)),
                  pl.BlockSpec((tm, tn), lambda i, j: (i, gb + j)),
                  pl.BlockSpec((d, tn), lambda i, j: (0, j)),
                  pl.BlockSpec((d, tn), lambda i, j: (0, j))],
        out_specs=pl.BlockSpec((tm, tn), lambda i, j: (i, j)),
        compiler_params=_cparams("parallel", "arbitrary"),
        name="merge_gate",
    )(a, b, px, px, wpa, wpb)


def _merge2_kernel(u_ref, wo_ref, x_ref, g_ref, mod_ref, o_ref, *, d):
    y = jnp.dot(u_ref[...], wo_ref[...], preferred_element_type=F32)
    n = y * lax.rsqrt(jnp.mean(y * y, axis=-1, keepdims=True) + EPS) * g_ref[...]
    o_ref[...] = x_ref[...] + mod_ref[:, 2 * d:3 * d] * n


def _merge2(u, wo, x2, g_post, mod3, row_of_tile, tm):
    t, d = x2.shape
    return pl.pallas_call(
        functools.partial(_merge2_kernel, d=d),
        out_shape=jax.ShapeDtypeStruct((t, d), F32),
        grid=(t // tm,),
        in_specs=[pl.BlockSpec((tm, d), lambda i: (i, 0)),
                  pl.BlockSpec((d, d), lambda i: (0, 0)),
                  pl.BlockSpec((tm, d), lambda i: (i, 0)),
                  pl.BlockSpec((1, d), lambda i: (0, 0)),
                  pl.BlockSpec((None, 1, 3 * d), lambda i: (row_of_tile(i), 0, 0))],
        out_specs=pl.BlockSpec((tm, d), lambda i: (i, 0)),
        compiler_params=_cparams("parallel"),
        name="merge_out",
    )(u, wo, x2, g_post.reshape(1, d), mod3)


def _rope_tables(seq):
    rows = seq // GRID_W
    r = jnp.repeat(jnp.arange(rows, dtype=F32), GRID_W)
    col = jnp.tile(jnp.arange(GRID_W, dtype=F32), rows)
    n_freq = HEAD_DIM // 4
    inv = ROPE_THETA ** (-jnp.arange(n_freq, dtype=F32) / n_freq)
    ang = jnp.concatenate([r[:, None] * inv, col[:, None] * inv], axis=-1)
    cos, sin = jnp.cos(ang), jnp.sin(ang)
    return jnp.concatenate([cos, cos], axis=-1), jnp.concatenate([-sin, sin], axis=-1)


def _tile_types(d, tn, rope):
    cols = _col_offsets(d)
    r = 1 if rope else 0
    kinds = {"k_a": r, "k_b": r, "q_a": r | 2 | 4, "q_b": r | 2 | 4}
    out = [0] * (cols["in_cols"] // tn)
    for name, ty in kinds.items():
        width = d // A_GROUP if name == "k_a" else d
        assert cols[name] % tn == 0 and width % tn == 0
        for t in range(cols[name] // tn, (cols[name] + width) // tn):
            out[t] = ty
    return jnp.asarray(out, jnp.int32)


def kernel(x, c, ctx, c_ctx, w_ada, b_ada, g_pre, g_post, w_in, sink, lam_qk, g_subln,
           w_proj_a, w_proj_b, w_out):
    batch, seq, d = x.shape
    ctx_len = ctx.shape[1]
    depth = w_in.shape[0]
    t_lat, t_ctx = batch * seq, batch * ctx_len
    cols = _col_offsets(d)
    kv_cols, in_cols = cols["kv_cols"], cols["in_cols"]

    tn_in = min(512, d // A_GROUP)
    tm_lat = _pick(seq, (1024, 512, 256))
    tm_in = _pick(seq, (2048, 1024, 512, 256))
    tm_ctx = _pick(t_ctx, (512, 256))
    tq_a = _pick(seq, (1024, 512, 256, 128))
    tq_b = _pick(seq, (1024, 512, 256))
    tk_b = _pick(seq // 2, (512, 256))
    tn_m = _pick(d // 2, (1024, 512))
    tm_m = _pick(seq, (512, 256))

    cosf, sinf = _rope_tables(seq)
    zeros_tab = jnp.zeros((tm_ctx, HEAD_DIM), F32)
    types_lat = _tile_types(d, tn_in, rope=True)
    types_ctx = _tile_types(d, tn_in, rope=False)
    cvec8 = jnp.concatenate([c, c_ctx[None, :], jnp.zeros((8 - batch - 1, d), F32)], axis=0)

    x2 = x.reshape(t_lat, d)
    c2 = ctx.reshape(t_ctx, d)
    lat_row = lambda tm: (lambda i: i // (seq // tm))
    ctx_row = lambda i: batch

    for l in range(depth):
        last = l == depth - 1
        lam_init = 0.8 - 0.6 * math.exp(-0.3 * l)
        wpa, wpb, wo = w_proj_a[l].astype(BF16), w_proj_b[l].astype(BF16), w_out[l].astype(BF16)
        mod3 = _ada(cvec8, w_ada, b_ada, l).reshape(8, 1, 3 * d)

        hx = _prenorm(x2, g_pre[l], mod3, lat_row(tm_lat), tm_lat)
        hc = _prenorm(c2, g_pre[l], mod3, ctx_row, tm_ctx)
        pxl = _inproj(hx, w_in, l, types_lat, cosf, sinf, ncols=in_cols,
                      tm=tm_in, tn=tn_in, pos_tiles=seq // tm_in)
        pxc = _inproj(hc, w_in, l, types_ctx, zeros_tab, zeros_tab, ncols=kv_cols if last else in_cols,
                      tm=tm_ctx, tn=tn_in, pos_tiles=1)

        o_a = _attn_a(pxl, pxc, sink[l], batch=batch, seq=seq, ctx_len=ctx_len, d=d, tq=tq_a)
        o_b = _attn_b_latent(pxl, pxc, lam_qk[l], g_subln[l], lam_init=lam_init, batch=batch,
                             seq=seq, ctx_len=ctx_len, d=d, tq=tq_b, tk=tk_b)
        u = _merge1(o_a, o_b, pxl, wpa, wpb, d=d, tm=tm_m, tn=tn_m)

        if not last:
            oc_a = _attn_a_ctx(pxc, sink[l], batch=batch, ctx_len=ctx_len, d=d)
            oc_b = _attn_b_ctx(pxc, lam_qk[l], g_subln[l], lam_init=lam_init, batch=batch,
                               ctx_len=ctx_len, d=d)
            uc = _merge1(oc_a, oc_b, pxc, wpa, wpb, d=d, tm=tm_ctx, tn=tn_m)
            c2 = _merge2(uc, wo, c2, g_post[l], mod3, ctx_row, tm_ctx)

        x2 = _merge2(u, wo, x2, g_post[l], mod3, lat_row(tm_m), tm_m)

    return x2.reshape(batch, seq, d)
```

```python
import functools
import math

import jax
import jax.numpy as jnp
from jax import lax
from jax.experimental import pallas as pl
from jax.experimental.pallas import tpu as pltpu

HEAD_DIM = 128
GRID_W = 64
WINDOW = 128
A_GROUP = 4
ROPE_THETA = 10000.0
EPS = 1e-6
NEG = -1e30
Q_SCALE = HEAD_DIM ** -0.5
LOG2E = math.log2(math.e)
VMEM_LIMIT_BYTES = 56 * 1024 * 1024

F32 = jnp.float32
BF16 = jnp.bfloat16


def _cparams(*semantics):
    return pltpu.CompilerParams(dimension_semantics=semantics, vmem_limit_bytes=VMEM_LIMIT_BYTES)


def _pick(n, candidates):
    for c in candidates:
        if n % c == 0:
            return c
    raise ValueError(f"no tile in {candidates} divides {n}")


def _sigmoid(v):
    return 0.5 * jnp.tanh(0.5 * v) + 0.5


def _silu(v):
    h = 0.5 * v
    return h * jnp.tanh(h) + h


def _gate_out(o, z, dtype):
    return (o * _silu(z)).astype(dtype)


def _col_offsets(d):
    kv = d // A_GROUP
    names = ("k_a", "v_a", "k_b", "v_b", "q_a", "z_a", "q_b", "z_b", "g_a", "g_b")
    widths = (kv, kv, d, d, d, d, d, d, d, d)
    off, out = 0, {}
    for n, w in zip(names, widths):
        out[n] = off
        off += w
    out["kv_cols"] = out["q_a"]
    out["in_cols"] = off
    return out


def _ada_kernel(c_ref, w_ref, b_ref, o_ref):
    a = _silu(c_ref[...])
    o_ref[...] = jnp.dot(a, w_ref[...], preferred_element_type=F32,
                         precision=lax.Precision.HIGHEST) + b_ref[...]


def _ada(cvec8, w_all, b_all, layer):
    depth, d, n = w_all.shape
    tn = _pick(n, (1024, 768, 512, 384, 256, 128))
    return pl.pallas_call(
        _ada_kernel,
        out_shape=jax.ShapeDtypeStruct((8, n), F32),
        grid=(n // tn,),
        in_specs=[pl.BlockSpec((8, d), lambda j: (0, 0)),
                  pl.BlockSpec((None, d, tn), lambda j: (layer, 0, j)),
                  pl.BlockSpec((None, 1, tn), lambda j: (layer, 0, j))],
        out_specs=pl.BlockSpec((8, tn), lambda j: (0, j)),
        compiler_params=_cparams("arbitrary"),
        name="ada_ln",
    )(cvec8, w_all, b_all.reshape(depth, 1, n))


def _prenorm_kernel(x_ref, g_ref, mod_ref, o_ref, *, d):
    x = x_ref[...]
    y = x * lax.rsqrt(jnp.mean(x * x, axis=-1, keepdims=True) + EPS)
    y = y * g_ref[...]
    shift = mod_ref[:, 0:d]
    scale = mod_ref[:, d:2 * d]
    o_ref[...] = (y * (1.0 + scale) + shift).astype(o_ref.dtype)


def _prenorm(x2, g, mod3, row_of_tile, tm):
    t, d = x2.shape
    return pl.pallas_call(
        functools.partial(_prenorm_kernel, d=d),
        out_shape=jax.ShapeDtypeStruct((t, d), BF16),
        grid=(t // tm,),
        in_specs=[pl.BlockSpec((tm, d), lambda i: (i, 0)),
                  pl.BlockSpec((1, d), lambda i: (0, 0)),
                  pl.BlockSpec((None, 1, 3 * d), lambda i: (row_of_tile(i), 0, 0))],
        out_specs=pl.BlockSpec((tm, d), lambda i: (i, 0)),
        compiler_params=_cparams("parallel"),
        name="prenorm",
    )(x2, g.reshape(1, d), mod3)


def _inproj_kernel(type_ref, h_ref, w_ref, cos_ref, sin_ref, o_ref):
    t = type_ref[pl.program_id(1)]
    heads = o_ref.shape[1] // HEAD_DIM

    def matmul():
        return jnp.dot(h_ref[...], w_ref[...].astype(h_ref.dtype), preferred_element_type=F32)

    @pl.when(t == 0)
    def _():
        o_ref[...] = matmul().astype(o_ref.dtype)

    @pl.when(t != 0)
    def _():
        acc = matmul()
        rope = (t & 1).astype(F32)
        scale = (jnp.where((t & 2) != 0, Q_SCALE, 1.0) * jnp.where((t & 4) != 0, LOG2E, 1.0)).astype(F32)
        ca = (rope * cos_ref[...] + (1.0 - rope)) * scale
        sa = (rope * sin_ref[...]) * scale
        for hh in range(heads):
            a = acc[:, hh * HEAD_DIM:(hh + 1) * HEAD_DIM]
            r = a * ca + pltpu.roll(a, HEAD_DIM // 2, 1) * sa
            o_ref[:, hh * HEAD_DIM:(hh + 1) * HEAD_DIM] = r.astype(o_ref.dtype)


def _inproj(h, w_all, layer, types, cosf, sinf, *, ncols, tm, tn, pos_tiles):
    t, d = h.shape
    grid_spec = pltpu.PrefetchScalarGridSpec(
        num_scalar_prefetch=1,
        grid=(t // tm, ncols // tn),
        in_specs=[pl.BlockSpec((tm, d), lambda i, j, ty: (i, 0)),
                  pl.BlockSpec((None, d, tn), lambda i, j, ty: (layer, 0, j)),
                  pl.BlockSpec((tm, HEAD_DIM), lambda i, j, ty: (i % pos_tiles, 0)),
                  pl.BlockSpec((tm, HEAD_DIM), lambda i, j, ty: (i % pos_tiles, 0))],
        out_specs=pl.BlockSpec((tm, tn), lambda i, j, ty: (i, j)),
    )
    return pl.pallas_call(
        _inproj_kernel,
        out_shape=jax.ShapeDtypeStruct((t, ncols), BF16),
        grid_spec=grid_spec,
        compiler_params=_cparams("parallel", "arbitrary"),
        name="in_proj",
    )(types, h, w_all, cosf, sinf)


def _stack_heads(q):
    g = q.shape[1] // HEAD_DIM
    return jnp.concatenate([q[:, i * HEAD_DIM:(i + 1) * HEAD_DIM] for i in range(g)], axis=0)


def _unstack_heads(o, g):
    rows = o.shape[0] // g
    return jnp.concatenate([o[i * rows:(i + 1) * rows, :] for i in range(g)], axis=1)


def _sink_column(sink_ref, kvh, rows):
    return jnp.concatenate(
        [jnp.full((rows, 1), sink_ref[kvh * A_GROUP + g], F32) for g in range(A_GROUP)], axis=0)


def _qkt(q, k):
    return lax.dot_general(q, k, (((1,), (1,)), ((), ())), preferred_element_type=F32)


def _attn_a_kernel(sink_ref, q_ref, kp_ref, km_ref, kn_ref, vp_ref, vm_ref, vn_ref, kc_ref, vc_ref,
                   z_ref, o_ref, *, seq):
    i = pl.program_id(1)
    kvh = pl.program_id(2)
    tq = q_ref.shape[0]
    k_win = jnp.concatenate([kp_ref[...], km_ref[...], kn_ref[...]], axis=0)
    v_win = jnp.concatenate([vp_ref[...], vm_ref[...], vn_ref[...]], axis=0)
    kc = kc_ref[...]
    vc = vc_ref[...]
    rows = A_GROUP * WINDOW
    nloc = 3 * WINDOW
    nkeys = nloc + kc.shape[0]
    sink = _sink_column(sink_ref, kvh, WINDOW) * LOG2E
    qi = lax.broadcasted_iota(jnp.int32, (rows, nkeys), 0) & (WINDOW - 1)
    kj = lax.broadcasted_iota(jnp.int32, (rows, nkeys), 1)
    band = jnp.abs(qi + WINDOW - kj) <= WINDOW
    is_ctx = kj >= nloc
    nsub = tq // WINDOW

    def scores(r):
        q = _stack_heads(q_ref[r * WINDOW:(r + 1) * WINDOW, :])
        k = jnp.concatenate([k_win[r * WINDOW:(r + 3) * WINDOW, :], kc], axis=0)
        base = i * tq + (r - 1) * WINDOW
        valid = is_ctx | (band & (kj >= -base) & (kj < seq - base))
        return jnp.where(valid, _qkt(q, k), NEG)

    def lane_bcast(col):
        return jnp.broadcast_to(col, (rows, HEAD_DIM))

    sink_b = lane_bcast(sink)

    def softmax(s):
        groups = [s[:, g * HEAD_DIM:(g + 1) * HEAD_DIM] for g in range(nkeys // HEAD_DIM)]
        gmax = functools.reduce(jnp.maximum, groups)
        m = jnp.maximum(lane_bcast(jnp.max(gmax, axis=-1, keepdims=True)), sink_b)
        ps = [jnp.exp2(sg - m) for sg in groups]
        psum = functools.reduce(jnp.add, ps)
        denom = lane_bcast(jnp.sum(psum, axis=-1, keepdims=True)) + jnp.exp2(sink_b - m)
        return jnp.concatenate([pg.astype(vc.dtype) for pg in ps], axis=1), denom

    def weigh(r, p, denom):
        v = jnp.concatenate([v_win[r * WINDOW:(r + 3) * WINDOW, :], vc], axis=0)
        o = _unstack_heads(jnp.dot(p, v, preferred_element_type=F32) / denom, A_GROUP)
        z = z_ref[r * WINDOW:(r + 1) * WINDOW, :].astype(F32)
        o_ref[r * WINDOW:(r + 1) * WINDOW, :] = _gate_out(o, z, o_ref.dtype)

    s_next = scores(0)
    pd_next = None
    for r in range(nsub + 2):
        pd_cur, pd_next = pd_next, None
        s_cur, s_next = s_next, None
        if r < nsub - 1:
            s_next = scores(r + 1)
        if s_cur is not None and r < nsub:
            pd_next = softmax(s_cur)
        if pd_cur is not None:
            weigh(r - 1, *pd_cur)


def _attn_a(pxl, pxc, sink, *, batch, seq, ctx_len, d, tq):
    kvh = d // HEAD_DIM // A_GROUP
    gw = A_GROUP * HEAD_DIM
    nq = seq // tq
    r = tq // WINDOW
    nblk = seq // WINDOW
    cols = _col_offsets(d)
    assert cols["q_a"] % gw == 0
    assert cols["z_a"] % gw == 0
    qa, za = cols["q_a"] // gw, cols["z_a"] // gw
    ka_l = ka_c = cols["k_a"] // HEAD_DIM
    va_l = va_c = cols["v_a"] // HEAD_DIM

    def prev(b, i, h):
        return (b * nblk + jnp.maximum(i * r - 1, 0), h)

    def nxt(b, i, h):
        return (b * nblk + jnp.minimum((i + 1) * r, nblk - 1), h)

    halo = (WINDOW, HEAD_DIM)
    return pl.pallas_call(
        functools.partial(_attn_a_kernel, seq=seq),
        out_shape=jax.ShapeDtypeStruct((batch * seq, d), BF16),
        grid=(batch, nq, kvh),
        in_specs=[
            pl.BlockSpec(memory_space=pltpu.SMEM),
            pl.BlockSpec((tq, gw), lambda b, i, h: (b * nq + i, qa + h)),
            pl.BlockSpec(halo, lambda b, i, h: prev(b, i, ka_l + h)),
            pl.BlockSpec((tq, HEAD_DIM), lambda b, i, h: (b * nq + i, ka_l + h)),
            pl.BlockSpec(halo, lambda b, i, h: nxt(b, i, ka_l + h)),
            pl.BlockSpec(halo, lambda b, i, h: prev(b, i, va_l + h)),
            pl.BlockSpec((tq, HEAD_DIM), lambda b, i, h: (b * nq + i, va_l + h)),
            pl.BlockSpec(halo, lambda b, i, h: nxt(b, i, va_l + h)),
            pl.BlockSpec((ctx_len, HEAD_DIM), lambda b, i, h: (b, ka_c + h)),
            pl.BlockSpec((ctx_len, HEAD_DIM), lambda b, i, h: (b, va_c + h)),
            pl.BlockSpec((tq, gw), lambda b, i, h: (b * nq + i, za + h)),
        ],
        out_specs=pl.BlockSpec((tq, gw), lambda b, i, h: (b * nq + i, h)),
        compiler_params=_cparams("parallel", "parallel", "arbitrary"),
        name="attn_a_latent",
    )(sink, pxl, pxl, pxl, pxl, pxl, pxl, pxl, pxc, pxc, pxl)


def _attn_a_ctx_kernel(sink_ref, q_ref, k_ref, v_ref, z_ref, o_ref):
    kvh = pl.program_id(1)
    rows = q_ref.shape[0]
    q = _stack_heads(q_ref[...])
    v = v_ref[...]
    sink = _sink_column(sink_ref, kvh, rows) * LOG2E
    s = _qkt(q, k_ref[...])
    m = jnp.maximum(jnp.max(s, axis=-1, keepdims=True), sink)
    p = jnp.exp2(s - m)
    denom = jnp.sum(p, axis=-1, keepdims=True) + jnp.exp2(sink - m)
    o = _unstack_heads(jnp.dot(p.astype(v.dtype), v, preferred_element_type=F32) / denom, A_GROUP)
    o_ref[...] = _gate_out(o, z_ref[...].astype(F32), o_ref.dtype)


def _attn_a_ctx(pxc, sink, *, batch, ctx_len, d):
    kvh = d // HEAD_DIM // A_GROUP
    gw = A_GROUP * HEAD_DIM
    cols = _col_offsets(d)
    qa, za = cols["q_a"] // gw, cols["z_a"] // gw
    ka_c, va_c = cols["k_a"] // HEAD_DIM, cols["v_a"] // HEAD_DIM
    return pl.pallas_call(
        _attn_a_ctx_kernel,
        out_shape=jax.ShapeDtypeStruct((batch * ctx_len, d), BF16),
        grid=(batch, kvh),
        in_specs=[pl.BlockSpec(memory_space=pltpu.SMEM),
                  pl.BlockSpec((ctx_len, gw), lambda b, h: (b, qa + h)),
                  pl.BlockSpec((ctx_len, HEAD_DIM), lambda b, h: (b, ka_c + h)),
                  pl.BlockSpec((ctx_len, HEAD_DIM), lambda b, h: (b, va_c + h)),
                  pl.BlockSpec((ctx_len, gw), lambda b, h: (b, za + h))],
        out_specs=pl.BlockSpec((ctx_len, gw), lambda b, h: (b, h)),
        compiler_params=_cparams("parallel", "arbitrary"),
        name="attn_a_context",
    )(sink, pxc, pxc, pxc, pxc)


def _row_sum_lanes(x):
    groups = [x[:, g * HEAD_DIM:(g + 1) * HEAD_DIM] for g in range(x.shape[1] // HEAD_DIM)]
    part = functools.reduce(jnp.add, groups)
    return jnp.broadcast_to(jnp.sum(part, axis=-1, keepdims=True), part.shape)


def _diff_finish(o0, l0, o1, l1, lamqk_ref, g_ref, lam_init):
    lq = lamqk_ref[...]
    lam = (jnp.exp(jnp.sum(lq[0:1] * lq[1:2], axis=-1, keepdims=True))
           - jnp.exp(jnp.sum(lq[2:3] * lq[3:4], axis=-1, keepdims=True)) + lam_init)
    wide = lambda t: jnp.concatenate([t] * (o0.shape[1] // HEAD_DIM), axis=1)
    o = o0 / wide(l0) - lam * (o1 / wide(l1))
    inv = lax.rsqrt(_row_sum_lanes(o * o) * (1.0 / o.shape[1]) + EPS)
    return o * wide(inv) * g_ref[...] * (1.0 - lam_init)


def _attn_b_ctx_kernel(q_ref, k_ref, v_ref, lamqk_ref, g_ref, z_ref, o_ref, *, lam_init):
    v = v_ref[...]
    parts = []
    for mp in range(2):
        cols = slice(mp * HEAD_DIM, (mp + 1) * HEAD_DIM)
        s = _qkt(q_ref[:, cols], k_ref[:, cols])
        p = jnp.exp2(s - jnp.max(s, axis=-1, keepdims=True))
        parts += [jnp.dot(p.astype(v.dtype), v, preferred_element_type=F32), _row_sum_lanes(p)]
    o = _diff_finish(*parts, lamqk_ref, g_ref, lam_init)
    o_ref[...] = _gate_out(o, z_ref[...].astype(F32), o_ref.dtype)


def _attn_b_ctx(pxc, lamqk, g_sub, *, lam_init, batch, ctx_len, d):
    hw = 2 * HEAD_DIM
    heads = d // hw
    cols = _col_offsets(d)
    qb, kb, vb, zb = cols["q_b"] // hw, cols["k_b"] // hw, cols["v_b"] // hw, cols["z_b"] // hw
    return pl.pallas_call(
        functools.partial(_attn_b_ctx_kernel, lam_init=lam_init),
        out_shape=jax.ShapeDtypeStruct((batch * ctx_len, d), BF16),
        grid=(batch, heads),
        in_specs=[pl.BlockSpec((ctx_len, hw), lambda b, h: (b, qb + h)),
                  pl.BlockSpec((ctx_len, hw), lambda b, h: (b, kb + h)),
                  pl.BlockSpec((ctx_len, hw), lambda b, h: (b, vb + h)),
                  pl.BlockSpec((4, HEAD_DIM), lambda b, h: (0, 0)),
                  pl.BlockSpec((1, hw), lambda b, h: (0, 0)),
                  pl.BlockSpec((ctx_len, hw), lambda b, h: (b, zb + h))],
        out_specs=pl.BlockSpec((ctx_len, hw), lambda b, h: (b, h)),
        compiler_params=_cparams("parallel", "arbitrary"),
        name="attn_b_context",
    )(pxc, pxc, pxc, lamqk, g_sub.reshape(1, hw), pxc)


def _attn_b_latent_kernel(q_ref, k_ref, v_ref, kc_ref, vc_ref, lamqk_ref, g_ref, z_ref, o_ref,
                          sa_scr, sb_scr, xa_scr, xb_scr, pa_scr, pb_scr, ala_scr, alb_scr,
                          m_scr, l_scr, acc_scr, *, lam_init, tk):
    nk = k_ref.shape[0] // tk
    tq = q_ref.shape[0]
    lanes = HEAD_DIM

    def lane_groups(x):
        return [x[:, g * lanes:(g + 1) * lanes] for g in range(x.shape[1] // lanes)]

    def group_max(s):
        return functools.reduce(jnp.maximum, lane_groups(s))

    def row_max(gmax):
        return jnp.broadcast_to(jnp.max(gmax, axis=-1, keepdims=True), (tq, lanes))

    def scores(c, buf):
        s_scr, x_scr = buf[0], buf[1]
        start = pl.multiple_of(c * tk, tk)
        for mp in range(2):
            cols = slice(mp * HEAD_DIM, (mp + 1) * HEAD_DIM)
            s = _qkt(q_ref[:, cols], k_ref[pl.ds(start, tk), cols]).astype(s_scr.dtype)
            s_scr[mp] = s
            x_scr[mp] = group_max(s).astype(F32)

    def softmax(buf):
        s_scr, x_scr, p_scr, al_scr = buf
        for mp in range(2):
            s = s_scr[mp]
            m_old = m_scr[mp]
            m_new = jnp.maximum(m_old, row_max(x_scr[mp]))
            alpha = jnp.exp2(m_old - m_new)
            m_low = m_new.astype(s.dtype)
            psum = None
            for g, sg in enumerate(lane_groups(s)):
                pg = jnp.exp2(sg - m_low)
                p_scr[mp, :, g * lanes:(g + 1) * lanes] = pg
                psum = pg if psum is None else psum + pg
            l_scr[mp] = alpha * l_scr[mp] + psum.astype(F32)
            al_scr[mp] = alpha
            m_scr[mp] = m_new

    def weigh(c, buf):
        p_scr, al_scr = buf[2], buf[3]
        v = v_ref[pl.ds(pl.multiple_of(c * tk, tk), tk), :]
        for mp in range(2):
            al = al_scr[mp]
            al = jnp.concatenate([al] * (acc_scr.shape[2] // lanes), axis=1)
            acc_scr[mp] = al * acc_scr[mp] + jnp.dot(p_scr[mp], v, preferred_element_type=F32)

    buf_a = (sa_scr, xa_scr, pa_scr, ala_scr)
    buf_b = (sb_scr, xb_scr, pb_scr, alb_scr)
    scores(0, buf_a)
    vc = vc_ref[...]
    for mp in range(2):
        cols = slice(mp * HEAD_DIM, (mp + 1) * HEAD_DIM)
        s = _qkt(q_ref[:, cols], kc_ref[:, cols])
        m0 = row_max(group_max(s)).astype(sa_scr.dtype).astype(F32)
        pgs = [jnp.exp2(sg - m0) for sg in lane_groups(s)]
        m_scr[mp] = m0
        l_scr[mp] = functools.reduce(jnp.add, pgs)
        acc_scr[mp] = jnp.dot(jnp.concatenate(pgs, axis=1).astype(vc.dtype), vc,
                              preferred_element_type=F32)
    softmax(buf_a)
    scores(1, buf_b)

    def steps(c0, n):
        for t in range(n):
            cur, nxt = (buf_a, buf_b) if t % 2 == 0 else (buf_b, buf_a)
            weigh(c0 + t, cur)
            softmax(nxt)
            scores(c0 + t + 2, cur)

    unroll = 4
    trips = (nk - 2) // unroll

    def body(i, carry):
        steps(i * unroll, unroll)
        return carry

    lax.fori_loop(0, trips, body, 0)
    steps(trips * unroll, nk - 2 - trips * unroll)
    weigh(nk - 2, buf_a)
    softmax(buf_b)
    weigh(nk - 1, buf_b)

    o = _diff_finish(acc_scr[0], _row_sum_lanes(l_scr[0]), acc_scr[1], _row_sum_lanes(l_scr[1]),
                     lamqk_ref, g_ref, lam_init)
    o_ref[...] = _gate_out(o, z_ref[...].astype(F32), o_ref.dtype)


def _attn_b_latent(pxl, pxc, lamqk, g_sub, *, lam_init, batch, seq, ctx_len, d, tq, tk):
    hw = 2 * HEAD_DIM
    heads = d // hw
    nq = seq // tq
    assert (seq // tk) % 2 == 0 and seq // tk >= 2
    cols = _col_offsets(d)
    qb, zb = cols["q_b"] // hw, cols["z_b"] // hw
    kb_l = kb_c = cols["k_b"] // hw
    vb_l = vb_c = cols["v_b"] // hw
    return pl.pallas_call(
        functools.partial(_attn_b_latent_kernel, lam_init=lam_init, tk=tk),
        out_shape=jax.ShapeDtypeStruct((batch * seq, d), BF16),
        grid=(batch, heads, nq),
        in_specs=[pl.BlockSpec((tq, hw), lambda b, h, i: (b * nq + i, qb + h)),
                  pl.BlockSpec((seq, hw), lambda b, h, i: (b, kb_l + h)),
                  pl.BlockSpec((seq, hw), lambda b, h, i: (b, vb_l + h)),
                  pl.BlockSpec((ctx_len, hw), lambda b, h, i: (b, kb_c + h)),
                  pl.BlockSpec((ctx_len, hw), lambda b, h, i: (b, vb_c + h)),
                  pl.BlockSpec((4, HEAD_DIM), lambda b, h, i: (0, 0)),
                  pl.BlockSpec((1, hw), lambda b, h, i: (0, 0)),
                  pl.BlockSpec((tq, hw), lambda b, h, i: (b * nq + i, zb + h))],
        out_specs=pl.BlockSpec((tq, hw), lambda b, h, i: (b * nq + i, h)),
        scratch_shapes=[pltpu.VMEM((2, tq, tk), BF16), pltpu.VMEM((2, tq, tk), BF16),
                        pltpu.VMEM((2, tq, HEAD_DIM), F32), pltpu.VMEM((2, tq, HEAD_DIM), F32),
                        pltpu.VMEM((2, tq, tk), BF16), pltpu.VMEM((2, tq, tk), BF16),
                        pltpu.VMEM((2, tq, HEAD_DIM), F32), pltpu.VMEM((2, tq, HEAD_DIM), F32),
                        pltpu.VMEM((2, tq, HEAD_DIM), F32), pltpu.VMEM((2, tq, HEAD_DIM), F32),
                        pltpu.VMEM((2, tq, hw), F32)],
        compiler_params=_cparams("parallel", "parallel", "arbitrary"),
        name="attn_b_latent",
    )(pxl, pxl, pxl, pxc, pxc, lamqk, g_sub.reshape(1, hw), pxl)


def _merge1_kernel(a_ref, b_ref, ga_ref, gb_ref, wa_ref, wb_ref, u_ref):
    ya = jnp.dot(a_ref[...], wa_ref[...], preferred_element_type=F32)
    yb = jnp.dot(b_ref[...], wb_ref[...], preferred_element_type=F32)
    u = _sigmoid(ga_ref[...].astype(F32)) * ya + _sigmoid(gb_ref[...].astype(F32)) * yb
    u_ref[...] = u.astype(u_ref.dtype)


def _merge1(a, b, px, wpa, wpb, *, d, tm, tn):
    t = a.shape[0]
    cols = _col_offsets(d)
    assert cols["g_a"] % tn == 0 and cols["g_b"] % tn == 0
    ga, gb = cols["g_a"] // tn, cols["g_b"] // tn
    return pl.pallas_call(
        _merge1_kernel,
        out_shape=jax.ShapeDtypeStruct((t, d), BF16),
        grid=(t // tm, d // tn),
        in_specs=[pl.BlockSpec((tm, d), lambda i, j: (i, 0)),
                  pl.BlockSpec((tm, d), lambda i, j: (i, 0)),
                  pl.BlockSpec((tm, tn), lambda i, j: (i, ga + j)),
                  pl.BlockSpec((tm, tn), lambda i, j: (i, gb + j)),
                  pl.BlockSpec((d, tn), lambda i, j: (0, j)),
                  pl.BlockSpec((d, tn), lambda i, j: (0, j))],
        out_specs=pl.BlockSpec((tm, tn), lambda i, j: (i, j)),
        compiler_params=_cparams("parallel", "arbitrary"),
        name="merge_gate",
    )(a, b, px, px, wpa, wpb)


def _merge2_kernel(u_ref, wo_ref, x_ref, g_ref, mod_ref, o_ref, *, d):
    y = jnp.dot(u_ref[...], wo_ref[...], preferred_element_type=F32)
    n = y * lax.rsqrt(jnp.mean(y * y, axis=-1, keepdims=True) + EPS) * g_ref[...]
    o_ref[...] = x_ref[...] + mod_ref[:, 2 * d:3 * d] * n


def _merge2(u, wo, x2, g_post, mod3, row_of_tile, tm):
    t, d = x2.shape
    return pl.pallas_call(
        functools.partial(_merge2_kernel, d=d),
        out_shape=jax.ShapeDtypeStruct((t, d), F32),
        grid=(t // tm,),
        in_specs=[pl.BlockSpec((tm, d), lambda i: (i, 0)),
                  pl.BlockSpec((d, d), lambda i: (0, 0)),
                  pl.BlockSpec((tm, d), lambda i: (i, 0)),
                  pl.BlockSpec((1, d), lambda i: (0, 0)),
                  pl.BlockSpec((None, 1, 3 * d), lambda i: (row_of_tile(i), 0, 0))],
        out_specs=pl.BlockSpec((tm, d), lambda i: (i, 0)),
        compiler_params=_cparams("parallel"),
        name="merge_out",
    )(u, wo, x2, g_post.reshape(1, d), mod3)


def _rope_tables(seq):
    rows = seq // GRID_W
    r = jnp.repeat(jnp.arange(rows, dtype=F32), GRID_W)
    col = jnp.tile(jnp.arange(GRID_W, dtype=F32), rows)
    n_freq = HEAD_DIM // 4
    inv = ROPE_THETA ** (-jnp.arange(n_freq, dtype=F32) / n_freq)
    ang = jnp.concatenate([r[:, None] * inv, col[:, None] * inv], axis=-1)
    cos, sin = jnp.cos(ang), jnp.sin(ang)
    return jnp.concatenate([cos, cos], axis=-1), jnp.concatenate([-sin, sin], axis=-1)


def _tile_types(d, tn, rope):
    cols = _col_offsets(d)
    r = 1 if rope else 0
    kinds = {"k_a": r, "k_b": r, "q_a": r | 2 | 4, "q_b": r | 2 | 4}
    out = [0] * (cols["in_cols"] // tn)
    for name, ty in kinds.items():
        width = d // A_GROUP if name == "k_a" else d
        assert cols[name] % tn == 0 and width % tn == 0
        for t in range(cols[name] // tn, (cols[name] + width) // tn):
            out[t] = ty
    return jnp.asarray(out, jnp.int32)


def kernel(x, c, ctx, c_ctx, w_ada, b_ada, g_pre, g_post, w_in, sink, lam_qk, g_subln,
           w_proj_a, w_proj_b, w_out):
    batch, seq, d = x.shape
    ctx_len = ctx.shape[1]
    depth = w_in.shape[0]
    t_lat, t_ctx = batch * seq, batch * ctx_len
    cols = _col_offsets(d)
    kv_cols, in_cols = cols["kv_cols"], cols["in_cols"]

    tn_in = min(512, d // A_GROUP)
    tm_lat = _pick(seq, (1024, 512, 256))
    tm_in = _pick(seq, (2048, 1024, 512, 256))
    tm_ctx = _pick(t_ctx, (512, 256))
    tq_a = _pick(seq, (1024, 512, 256, 128))
    tq_b = _pick(seq, (1024, 512, 256))
    tk_b = _pick(seq // 2, (512, 256))
    tn_m = _pick(d // 2, (1024, 512))
    tm_m = _pick(seq, (512, 256))

    cosf, sinf = _rope_tables(seq)
    zeros_tab = jnp.zeros((tm_ctx, HEAD_DIM), F32)
    types_lat = _tile_types(d, tn_in, rope=True)
    types_ctx = _tile_types(d, tn_in, rope=False)
    cvec8 = jnp.concatenate([c, c_ctx[None, :], jnp.zeros((8 - batch - 1, d), F32)], axis=0)

    x2 = x.reshape(t_lat, d)
    c2 = ctx.reshape(t_ctx, d)
    lat_row = lambda tm: (lambda i: i // (seq // tm))
    ctx_row = lambda i: batch

    for l in range(depth):
        last = l == depth - 1
        lam_init = 0.8 - 0.6 * math.exp(-0.3 * l)
        wpa, wpb, wo = w_proj_a[l].astype(BF16), w_proj_b[l].astype(BF16), w_out[l].astype(BF16)
        mod3 = _ada(cvec8, w_ada, b_ada, l).reshape(8, 1, 3 * d)

        hx = _prenorm(x2, g_pre[l], mod3, lat_row(tm_lat), tm_lat)
        hc = _prenorm(c2, g_pre[l], mod3, ctx_row, tm_ctx)
        pxl = _inproj(hx, w_in, l, types_lat, cosf, sinf, ncols=in_cols,
                      tm=tm_in, tn=tn_in, pos_tiles=seq // tm_in)
        pxc = _inproj(hc, w_in, l, types_ctx, zeros_tab, zeros_tab, ncols=kv_cols if last else in_cols,
                      tm=tm_ctx, tn=tn_in, pos_tiles=1)

        o_a = _attn_a(pxl, pxc, sink[l], batch=batch, seq=seq, ctx_len=ctx_len, d=d, tq=tq_a)
        o_b = _attn_b_latent(pxl, pxc, lam_qk[l], g_subln[l], lam_init=lam_init, batch=batch,
                             seq=seq, ctx_len=ctx_len, d=d, tq=tq_b, tk=tk_b)
        u = _merge1(o_a, o_b, pxl, wpa, wpb, d=d, tm=tm_m, tn=tn_m)

        if not last:
            oc_a = _attn_a_ctx(pxc, sink[l], batch=batch, ctx_len=ctx_len, d=d)
            oc_b = _attn_b_ctx(pxc, lam_qk[l], g_subln[l], lam_init=lam_init, batch=batch,
                               ctx_len=ctx_len, d=d)
            uc = _merge1(oc_a, oc_b, pxc, wpa, wpb, d=d, tm=tm_ctx, tn=tn_m)
            c2 = _merge2(uc, wo, c2, g_post[l], mod3, ctx_row, tm_ctx)

        x2 = _merge2(u, wo, x2, g_post[l], mod3, lat_row(tm_m), tm_m)

    return x2.reshape(batch, seq, d)
```

```python
import functools
import math

import jax
import jax.numpy as jnp
from jax import lax
from jax.experimental import pallas as pl
from jax.experimental.pallas import tpu as pltpu

HEAD_DIM = 128
GRID_W = 64
WINDOW = 128
A_GROUP = 4
ROPE_THETA = 10000.0
EPS = 1e-6
NEG = -1e30
Q_SCALE = HEAD_DIM ** -0.5
LOG2E = math.log2(math.e)
VMEM_LIMIT_BYTES = 56 * 1024 * 1024

F32 = jnp.float32
BF16 = jnp.bfloat16


def _cparams(*semantics):
    return pltpu.CompilerParams(dimension_semantics=semantics, vmem_limit_bytes=VMEM_LIMIT_BYTES)


def _pick(n, candidates):
    for c in candidates:
        if n % c == 0:
            return c
    raise ValueError(f"no tile in {candidates} divides {n}")


def _sigmoid(v):
    return 0.5 * jnp.tanh(0.5 * v) + 0.5


def _silu(v):
    h = 0.5 * v
    return h * jnp.tanh(h) + h


def _gate_out(o, z, dtype):
    return (o * _silu(z)).astype(dtype)


def _col_offsets(d):
    kv = d // A_GROUP
    names = ("k_a", "v_a", "k_b", "v_b", "q_a", "z_a", "q_b", "z_b", "g_a", "g_b")
    widths = (kv, kv, d, d, d, d, d, d, d, d)
    off, out = 0, {}
    for n, w in zip(names, widths):
        out[n] = off
        off += w
    out["kv_cols"] = out["q_a"]
    out["in_cols"] = off
    return out


def _ada_kernel(c_ref, w_ref, b_ref, o_ref):
    a = _silu(c_ref[...])
    o_ref[...] = jnp.dot(a, w_ref[...], preferred_element_type=F32,
                         precision=lax.Precision.HIGHEST) + b_ref[...]


def _ada(cvec8, w_all, b_all, layer):
    depth, d, n = w_all.shape
    tn = _pick(n, (1024, 768, 512, 384, 256, 128))
    return pl.pallas_call(
        _ada_kernel,
        out_shape=jax.ShapeDtypeStruct((8, n), F32),
        grid=(n // tn,),
        in_specs=[pl.BlockSpec((8, d), lambda j: (0, 0)),
                  pl.BlockSpec((None, d, tn), lambda j: (layer, 0, j)),
                  pl.BlockSpec((None, 1, tn), lambda j: (layer, 0, j))],
        out_specs=pl.BlockSpec((8, tn), lambda j: (0, j)),
        compiler_params=_cparams("arbitrary"),
        name="ada_ln",
    )(cvec8, w_all, b_all.reshape(depth, 1, n))


def _prenorm_kernel(x_ref, g_ref, mod_ref, o_ref, *, d):
    x = x_ref[...]
    y = x * lax.rsqrt(jnp.mean(x * x, axis=-1, keepdims=True) + EPS)
    y = y * g_ref[...]
    shift = mod_ref[:, 0:d]
    scale = mod_ref[:, d:2 * d]
    o_ref[...] = (y * (1.0 + scale) + shift).astype(o_ref.dtype)


def _prenorm(x2, g, mod3, row_of_tile, tm):
    t, d = x2.shape
    return pl.pallas_call(
        functools.partial(_prenorm_kernel, d=d),
        out_shape=jax.ShapeDtypeStruct((t, d), BF16),
        grid=(t // tm,),
        in_specs=[pl.BlockSpec((tm, d), lambda i: (i, 0)),
                  pl.BlockSpec((1, d), lambda i: (0, 0)),
                  pl.BlockSpec((None, 1, 3 * d), lambda i: (row_of_tile(i), 0, 0))],
        out_specs=pl.BlockSpec((tm, d), lambda i: (i, 0)),
        compiler_params=_cparams("parallel"),
        name="prenorm",
    )(x2, g.reshape(1, d), mod3)


def _inproj_kernel(type_ref, h_ref, w_ref, cos_ref, sin_ref, o_ref):
    t = type_ref[pl.program_id(1)]
    heads = o_ref.shape[1] // HEAD_DIM

    def matmul():
        return jnp.dot(h_ref[...], w_ref[...].astype(h_ref.dtype), preferred_element_type=F32)

    @pl.when(t == 0)
    def _():
        o_ref[...] = matmul().astype(o_ref.dtype)

    @pl.when(t != 0)
    def _():
        acc = matmul()
        rope = (t & 1).astype(F32)
        scale = (jnp.where((t & 2) != 0, Q_SCALE, 1.0) * jnp.where((t & 4) != 0, LOG2E, 1.0)).astype(F32)
        ca = (rope * cos_ref[...] + (1.0 - rope)) * scale
        sa = (rope * sin_ref[...]) * scale
        for hh in range(heads):
            a = acc[:, hh * HEAD_DIM:(hh + 1) * HEAD_DIM]
            r = a * ca + pltpu.roll(a, HEAD_DIM // 2, 1) * sa
            o_ref[:, hh * HEAD_DIM:(hh + 1) * HEAD_DIM] = r.astype(o_ref.dtype)


def _inproj(h, w_all, layer, types, cosf, sinf, *, ncols, tm, tn, pos_tiles):
    t, d = h.shape
    grid_spec = pltpu.PrefetchScalarGridSpec(
        num_scalar_prefetch=1,
        grid=(t // tm, ncols // tn),
        in_specs=[pl.BlockSpec((tm, d), lambda i, j, ty: (i, 0)),
                  pl.BlockSpec((None, d, tn), lambda i, j, ty: (layer, 0, j)),
                  pl.BlockSpec((tm, HEAD_DIM), lambda i, j, ty: (i % pos_tiles, 0)),
                  pl.BlockSpec((tm, HEAD_DIM), lambda i, j, ty: (i % pos_tiles, 0))],
        out_specs=pl.BlockSpec((tm, tn), lambda i, j, ty: (i, j)),
    )
    return pl.pallas_call(
        _inproj_kernel,
        out_shape=jax.ShapeDtypeStruct((t, ncols), BF16),
        grid_spec=grid_spec,
        compiler_params=_cparams("parallel", "arbitrary"),
        name="in_proj",
    )(types, h, w_all, cosf, sinf)


def _stack_heads(q):
    g = q.shape[1] // HEAD_DIM
    return jnp.concatenate([q[:, i * HEAD_DIM:(i + 1) * HEAD_DIM] for i in range(g)], axis=0)


def _unstack_heads(o, g):
    rows = o.shape[0] // g
    return jnp.concatenate([o[i * rows:(i + 1) * rows, :] for i in range(g)], axis=1)


def _sink_column(sink_ref, kvh, rows):
    return jnp.concatenate(
        [jnp.full((rows, 1), sink_ref[kvh * A_GROUP + g], F32) for g in range(A_GROUP)], axis=0)


def _qkt(q, k):
    return lax.dot_general(q, k, (((1,), (1,)), ((), ())), preferred_element_type=F32)


def _attn_a_kernel(sink_ref, q_ref, kp_ref, km_ref, kn_ref, vp_ref, vm_ref, vn_ref, kc_ref, vc_ref,
                   z_ref, o_ref, *, seq):
    i = pl.program_id(1)
    kvh = pl.program_id(2)
    tq = q_ref.shape[0]
    k_win = jnp.concatenate([kp_ref[...], km_ref[...], kn_ref[...]], axis=0)
    v_win = jnp.concatenate([vp_ref[...], vm_ref[...], vn_ref[...]], axis=0)
    kc = kc_ref[...]
    vc = vc_ref[...]
    rows = A_GROUP * WINDOW
    nloc = 3 * WINDOW
    nkeys = nloc + kc.shape[0]
    sink = _sink_column(sink_ref, kvh, WINDOW) * LOG2E
    qi = lax.broadcasted_iota(jnp.int32, (rows, nkeys), 0) & (WINDOW - 1)
    kj = lax.broadcasted_iota(jnp.int32, (rows, nkeys), 1)
    band = jnp.abs(qi + WINDOW - kj) <= WINDOW
    is_ctx = kj >= nloc
    nsub = tq // WINDOW

    def scores(r):
        q = _stack_heads(q_ref[r * WINDOW:(r + 1) * WINDOW, :])
        k = jnp.concatenate([k_win[r * WINDOW:(r + 3) * WINDOW, :], kc], axis=0)
        base = i * tq + (r - 1) * WINDOW
        valid = is_ctx | (band & (kj >= -base) & (kj < seq - base))
        return jnp.where(valid, _qkt(q, k), NEG)

    def lane_bcast(col):
        return jnp.broadcast_to(col, (rows, HEAD_DIM))

    sink_b = lane_bcast(sink)

    def softmax(s):
        groups = [s[:, g * HEAD_DIM:(g + 1) * HEAD_DIM] for g in range(nkeys // HEAD_DIM)]
        gmax = functools.reduce(jnp.maximum, groups)
        m = jnp.maximum(lane_bcast(jnp.max(gmax, axis=-1, keepdims=True)), sink_b)
        ps = [jnp.exp2(sg - m) for sg in groups]
        psum = functools.reduce(jnp.add, ps)
        denom = lane_bcast(jnp.sum(psum, axis=-1, keepdims=True)) + jnp.exp2(sink_b - m)
        return jnp.concatenate([pg.astype(vc.dtype) for pg in ps], axis=1), denom

    def weigh(r, p, denom):
        v = jnp.concatenate([v_win[r * WINDOW:(r + 3) * WINDOW, :], vc], axis=0)
        o = _unstack_heads(jnp.dot(p, v, preferred_element_type=F32) / denom, A_GROUP)
        z = z_ref[r * WINDOW:(r + 1) * WINDOW, :].astype(F32)
        o_ref[r * WINDOW:(r + 1) * WINDOW, :] = _gate_out(o, z, o_ref.dtype)

    s_next = scores(0)
    pd_next = None
    for r in range(nsub + 2):
        pd_cur, pd_next = pd_next, None
        s_cur, s_next = s_next, None
        if r < nsub - 1:
            s_next = scores(r + 1)
        if s_cur is not None and r < nsub:
            pd_next = softmax(s_cur)
        if pd_cur is not None:
            weigh(r - 1, *pd_cur)


def _attn_a(pxl, pxc, sink, *, batch, seq, ctx_len, d, tq):
    kvh = d // HEAD_DIM // A_GROUP
    gw = A_GROUP * HEAD_DIM
    nq = seq // tq
    r = tq // WINDOW
    nblk = seq // WINDOW
    cols = _col_offsets(d)
    assert cols["q_a"] % gw == 0
    assert cols["z_a"] % gw == 0
    qa, za = cols["q_a"] // gw, cols["z_a"] // gw
    ka_l = ka_c = cols["k_a"] // HEAD_DIM
    va_l = va_c = cols["v_a"] // HEAD_DIM

    def prev(b, i, h):
        return (b * nblk + jnp.maximum(i * r - 1, 0), h)

    def nxt(b, i, h):
        return (b * nblk + jnp.minimum((i + 1) * r, nblk - 1), h)

    halo = (WINDOW, HEAD_DIM)
    return pl.pallas_call(
        functools.partial(_attn_a_kernel, seq=seq),
        out_shape=jax.ShapeDtypeStruct((batch * seq, d), BF16),
        grid=(batch, nq, kvh),
        in_specs=[
            pl.BlockSpec(memory_space=pltpu.SMEM),
            pl.BlockSpec((tq, gw), lambda b, i, h: (b * nq + i, qa + h)),
            pl.BlockSpec(halo, lambda b, i, h: prev(b, i, ka_l + h)),
            pl.BlockSpec((tq, HEAD_DIM), lambda b, i, h: (b * nq + i, ka_l + h)),
            pl.BlockSpec(halo, lambda b, i, h: nxt(b, i, ka_l + h)),
            pl.BlockSpec(halo, lambda b, i, h: prev(b, i, va_l + h)),
            pl.BlockSpec((tq, HEAD_DIM), lambda b, i, h: (b * nq + i, va_l + h)),
            pl.BlockSpec(halo, lambda b, i, h: nxt(b, i, va_l + h)),
            pl.BlockSpec((ctx_len, HEAD_DIM), lambda b, i, h: (b, ka_c + h)),
            pl.BlockSpec((ctx_len, HEAD_DIM), lambda b, i, h: (b, va_c + h)),
            pl.BlockSpec((tq, gw), lambda b, i, h: (b * nq + i, za + h)),
        ],
        out_specs=pl.BlockSpec((tq, gw), lambda b, i, h: (b * nq + i, h)),
        compiler_params=_cparams("parallel", "parallel", "arbitrary"),
        name="attn_a_latent",
    )(sink, pxl, pxl, pxl, pxl, pxl, pxl, pxl, pxc, pxc, pxl)


def _attn_a_ctx_kernel(sink_ref, q_ref, k_ref, v_ref, z_ref, o_ref):
    kvh = pl.program_id(1)
    rows = q_ref.shape[0]
    q = _stack_heads(q_ref[...])
    v = v_ref[...]
    sink = _sink_column(sink_ref, kvh, rows) * LOG2E
    s = _qkt(q, k_ref[...])
    m = jnp.maximum(jnp.max(s, axis=-1, keepdims=True), sink)
    p = jnp.exp2(s - m)
    denom = jnp.sum(p, axis=-1, keepdims=True) + jnp.exp2(sink - m)
    o = _unstack_heads(jnp.dot(p.astype(v.dtype), v, preferred_element_type=F32) / denom, A_GROUP)
    o_ref[...] = _gate_out(o, z_ref[...].astype(F32), o_ref.dtype)


def _attn_a_ctx(pxc, sink, *, batch, ctx_len, d):
    kvh = d // HEAD_DIM // A_GROUP
    gw = A_GROUP * HEAD_DIM
    cols = _col_offsets(d)
    qa, za = cols["q_a"] // gw, cols["z_a"] // gw
    ka_c, va_c = cols["k_a"] // HEAD_DIM, cols["v_a"] // HEAD_DIM
    return pl.pallas_call(
        _attn_a_ctx_kernel,
        out_shape=jax.ShapeDtypeStruct((batch * ctx_len, d), BF16),
        grid=(batch, kvh),
        in_specs=[pl.BlockSpec(memory_space=pltpu.SMEM),
                  pl.BlockSpec((ctx_len, gw), lambda b, h: (b, qa + h)),
                  pl.BlockSpec((ctx_len, HEAD_DIM), lambda b, h: (b, ka_c + h)),
                  pl.BlockSpec((ctx_len, HEAD_DIM), lambda b, h: (b, va_c + h)),
                  pl.BlockSpec((ctx_len, gw), lambda b, h: (b, za + h))],
        out_specs=pl.BlockSpec((ctx_len, gw), lambda b, h: (b, h)),
        compiler_params=_cparams("parallel", "arbitrary"),
        name="attn_a_context",
    )(sink, pxc, pxc, pxc, pxc)


def _row_sum_lanes(x):
    groups = [x[:, g * HEAD_DIM:(g + 1) * HEAD_DIM] for g in range(x.shape[1] // HEAD_DIM)]
    part = functools.reduce(jnp.add, groups)
    return jnp.broadcast_to(jnp.sum(part, axis=-1, keepdims=True), part.shape)


def _diff_finish(o0, l0, o1, l1, lamqk_ref, g_ref, lam_init):
    lq = lamqk_ref[...]
    lam = (jnp.exp(jnp.sum(lq[0:1] * lq[1:2], axis=-1, keepdims=True))
           - jnp.exp(jnp.sum(lq[2:3] * lq[3:4], axis=-1, keepdims=True)) + lam_init)
    wide = lambda t: jnp.concatenate([t] * (o0.shape[1] // HEAD_DIM), axis=1)
    o = o0 / wide(l0) - lam * (o1 / wide(l1))
    inv = lax.rsqrt(_row_sum_lanes(o * o) * (1.0 / o.shape[1]) + EPS)
    return o * wide(inv) * g_ref[...] * (1.0 - lam_init)


def _attn_b_ctx_kernel(q_ref, k_ref, v_ref, lamqk_ref, g_ref, z_ref, o_ref, *, lam_init):
    v = v_ref[...]
    parts = []
    for mp in range(2):
        cols = slice(mp * HEAD_DIM, (mp + 1) * HEAD_DIM)
        s = _qkt(q_ref[:, cols], k_ref[:, cols])
        p = jnp.exp2(s - jnp.max(s, axis=-1, keepdims=True))
        parts += [jnp.dot(p.astype(v.dtype), v, preferred_element_type=F32), _row_sum_lanes(p)]
    o = _diff_finish(*parts, lamqk_ref, g_ref, lam_init)
    o_ref[...] = _gate_out(o, z_ref[...].astype(F32), o_ref.dtype)


def _attn_b_ctx(pxc, lamqk, g_sub, *, lam_init, batch, ctx_len, d):
    hw = 2 * HEAD_DIM
    heads = d // hw
    cols = _col_offsets(d)
    qb, kb, vb, zb = cols["q_b"] // hw, cols["k_b"] // hw, cols["v_b"] // hw, cols["z_b"] // hw
    return pl.pallas_call(
        functools.partial(_attn_b_ctx_kernel, lam_init=lam_init),
        out_shape=jax.ShapeDtypeStruct((batch * ctx_len, d), BF16),
        grid=(batch, heads),
        in_specs=[pl.BlockSpec((ctx_len, hw), lambda b, h: (b, qb + h)),
                  pl.BlockSpec((ctx_len, hw), lambda b, h: (b, kb + h)),
                  pl.BlockSpec((ctx_len, hw), lambda b, h: (b, vb + h)),
                  pl.BlockSpec((4, HEAD_DIM), lambda b, h: (0, 0)),
                  pl.BlockSpec((1, hw), lambda b, h: (0, 0)),
                  pl.BlockSpec((ctx_len, hw), lambda b, h: (b, zb + h))],
        out_specs=pl.BlockSpec((ctx_len, hw), lambda b, h: (b, h)),
        compiler_params=_cparams("parallel", "arbitrary"),
        name="attn_b_context",
    )(pxc, pxc, pxc, lamqk, g_sub.reshape(1, hw), pxc)


def _attn_b_latent_kernel(q_ref, k_ref, v_ref, kc_ref, vc_ref, lamqk_ref, g_ref, z_ref, o_ref,
                          sa_scr, sb_scr, xa_scr, xb_scr, pa_scr, pb_scr, ala_scr, alb_scr,
                          m_scr, l_scr, acc_scr, *, lam_init, tk):
    nk = k_ref.shape[0] // tk
    tq = q_ref.shape[0]
    lanes = HEAD_DIM

    def lane_groups(x):
        return [x[:, g * lanes:(g + 1) * lanes] for g in range(x.shape[1] // lanes)]

    def group_max(s):
        return functools.reduce(jnp.maximum, lane_groups(s))

    def row_max(gmax):
        return jnp.broadcast_to(jnp.max(gmax, axis=-1, keepdims=True), (tq, lanes))

    nctx = kc_ref.shape[0]

    def scores(c, buf, with_ctx=False):
        s_scr, x_scr = buf[0], buf[1]
        start = pl.multiple_of(c * tk, tk)
        for mp in range(2):
            cols = slice(mp * HEAD_DIM, (mp + 1) * HEAD_DIM)
            s = _qkt(q_ref[:, cols], k_ref[pl.ds(start, tk), cols])
            s_scr[mp, :, 0:tk] = s
            gmax = group_max(s)
            if with_ctx:
                sc = _qkt(q_ref[:, cols], kc_ref[:, cols])
                s_scr[mp, :, tk:tk + nctx] = sc
                gmax = jnp.maximum(gmax, group_max(sc))
            x_scr[mp] = gmax

    def softmax(buf, first=False):
        s_scr, x_scr, p_scr, al_scr = buf
        width = tk + nctx if first else tk
        for mp in range(2):
            m_new = row_max(x_scr[mp])
            if not first:
                m_old = m_scr[mp]
                m_new = jnp.maximum(m_old, m_new)
                alpha = jnp.exp2(m_old - m_new)
            psum = None
            for g in range(width // lanes):
                cols = slice(g * lanes, (g + 1) * lanes)
                pg = jnp.exp2(s_scr[mp, :, cols] - m_new)
                p_scr[mp, :, cols] = pg.astype(p_scr.dtype)
                psum = pg if psum is None else psum + pg
            if first:
                l_scr[mp] = psum
            else:
                l_scr[mp] = alpha * l_scr[mp] + psum
                al_scr[mp] = alpha
            m_scr[mp] = m_new

    def weigh(c, buf, first=False):
        p_scr, al_scr = buf[2], buf[3]
        v = v_ref[pl.ds(pl.multiple_of(c * tk, tk), tk), :]
        for mp in range(2):
            pv = jnp.dot(p_scr[mp, :, 0:tk], v, preferred_element_type=F32)
            if first:
                acc_scr[mp] = pv + jnp.dot(p_scr[mp, :, tk:tk + nctx], vc_ref[...],
                                           preferred_element_type=F32)
            else:
                al = jnp.concatenate([al_scr[mp]] * (acc_scr.shape[2] // lanes), axis=1)
                acc_scr[mp] = al * acc_scr[mp] + pv

    buf_a = (sa_scr, xa_scr, pa_scr, ala_scr)
    buf_b = (sb_scr, xb_scr, pb_scr, alb_scr)
    bufs = (buf_a, buf_b)

    def steps(c0, n, parity):
        for t in range(n):
            cur, nxt = bufs[(parity + t) % 2], bufs[(parity + t + 1) % 2]
            weigh(c0 + t, cur)
            softmax(nxt)
            scores(c0 + t + 2, cur)

    scores(0, buf_a, with_ctx=True)
    softmax(buf_a, first=True)
    scores(1, buf_b)
    weigh(0, buf_a, first=True)
    softmax(buf_b)
    scores(2, buf_a)

    unroll = 4
    trips = (nk - 3) // unroll

    def body(i, carry):
        steps(1 + i * unroll, unroll, 1)
        return carry

    lax.fori_loop(0, trips, body, 0)
    steps(1 + trips * unroll, nk - 3 - trips * unroll, 1)
    weigh(nk - 2, buf_a)
    softmax(buf_b)
    weigh(nk - 1, buf_b)

    o = _diff_finish(acc_scr[0], _row_sum_lanes(l_scr[0]), acc_scr[1], _row_sum_lanes(l_scr[1]),
                     lamqk_ref, g_ref, lam_init)
    o_ref[...] = _gate_out(o, z_ref[...].astype(F32), o_ref.dtype)


def _attn_b_latent(pxl, pxc, lamqk, g_sub, *, lam_init, batch, seq, ctx_len, d, tq, tk):
    hw = 2 * HEAD_DIM
    heads = d // hw
    nq = seq // tq
    assert (seq // tk) % 2 == 0 and seq // tk >= 4 and ctx_len % HEAD_DIM == 0
    cols = _col_offsets(d)
    qb, zb = cols["q_b"] // hw, cols["z_b"] // hw
    kb_l = kb_c = cols["k_b"] // hw
    vb_l = vb_c = cols["v_b"] // hw
    return pl.pallas_call(
        functools.partial(_attn_b_latent_kernel, lam_init=lam_init, tk=tk),
        out_shape=jax.ShapeDtypeStruct((batch * seq, d), BF16),
        grid=(batch, heads, nq),
        in_specs=[pl.BlockSpec((tq, hw), lambda b, h, i: (b * nq + i, qb + h)),
                  pl.BlockSpec((seq, hw), lambda b, h, i: (b, kb_l + h)),
                  pl.BlockSpec((seq, hw), lambda b, h, i: (b, vb_l + h)),
                  pl.BlockSpec((ctx_len, hw), lambda b, h, i: (b, kb_c + h)),
                  pl.BlockSpec((ctx_len, hw), lambda b, h, i: (b, vb_c + h)),
                  pl.BlockSpec((4, HEAD_DIM), lambda b, h, i: (0, 0)),
                  pl.BlockSpec((1, hw), lambda b, h, i: (0, 0)),
                  pl.BlockSpec((tq, hw), lambda b, h, i: (b * nq + i, zb + h))],
        out_specs=pl.BlockSpec((tq, hw), lambda b, h, i: (b * nq + i, h)),
        scratch_shapes=[pltpu.VMEM((2, tq, tk + ctx_len), F32), pltpu.VMEM((2, tq, tk), F32),
                        pltpu.VMEM((2, tq, HEAD_DIM), F32), pltpu.VMEM((2, tq, HEAD_DIM), F32),
                        pltpu.VMEM((2, tq, tk + ctx_len), BF16), pltpu.VMEM((2, tq, tk), BF16),
                        pltpu.VMEM((2, tq, HEAD_DIM), F32), pltpu.VMEM((2, tq, HEAD_DIM), F32),
                        pltpu.VMEM((2, tq, HEAD_DIM), F32), pltpu.VMEM((2, tq, HEAD_DIM), F32),
                        pltpu.VMEM((2, tq, hw), F32)],
        compiler_params=_cparams("parallel", "parallel", "arbitrary"),
        name="attn_b_latent",
    )(pxl, pxl, pxl, pxc, pxc, lamqk, g_sub.reshape(1, hw), pxl)


def _merge1_kernel(a_ref, b_ref, ga_ref, gb_ref, wa_ref, wb_ref, u_ref):
    ya = jnp.dot(a_ref[...], wa_ref[...], preferred_element_type=F32)
    yb = jnp.dot(b_ref[...], wb_ref[...], preferred_element_type=F32)
    u = _sigmoid(ga_ref[...].astype(F32)) * ya + _sigmoid(gb_ref[...].astype(F32)) * yb
    u_ref[...] = u.astype(u_ref.dtype)


def _merge1(a, b, px, wpa, wpb, *, d, tm, tn):
    t = a.shape[0]
    cols = _col_offsets(d)
    assert cols["g_a"] % tn == 0 and cols["g_b"] % tn == 0
    ga, gb = cols["g_a"] // tn, cols["g_b"] // tn
    return pl.pallas_call(
        _merge1_kernel,
        out_shape=jax.ShapeDtypeStruct((t, d), BF16),
        grid=(t // tm, d // tn),
        in_specs=[pl.BlockSpec((tm, d), lambda i, j: (i, 0)),
                  pl.BlockSpec((tm, d), lambda i, j: (i, 0)),
                  pl.BlockSpec((tm, tn), lambda i, j: (i, ga + j)),
                  pl.BlockSpec((tm, tn), lambda i, j: (i, gb + j)),
                  pl.BlockSpec((d, tn), lambda i, j: (0, j)),
                  pl.BlockSpec((d, tn), lambda i, j: (0, j))],
        out_specs=pl.BlockSpec((tm, tn), lambda i, j: (i, j)),
        compiler_params=_cparams("parallel", "arbitrary"),
        name="merge_gate",
    )(a, b, px, px, wpa, wpb)


def _merge2_kernel(u_ref, wo_ref, x_ref, g_ref, mod_ref, o_ref, *, d):
    y = jnp.dot(u_ref[...], wo_ref[...], preferred_element_type=F32)
    n = y * lax.rsqrt(jnp.mean(y * y, axis=-1, keepdims=True) + EPS) * g_ref[...]
    o_ref[...] = x_ref[...] + mod_ref[:, 2 * d:3 * d] * n


def _merge2(u, wo, x2, g_post, mod3, row_of_tile, tm):
    t, d = x2.shape
    return pl.pallas_call(
        functools.partial(_merge2_kernel, d=d),
        out_shape=jax.ShapeDtypeStruct((t, d), F32),
        grid=(t // tm,),
        in_specs=[pl.BlockSpec((tm, d), lambda i: (i, 0)),
                  pl.BlockSpec((d, d), lambda i: (0, 0)),
                  pl.BlockSpec((tm, d), lambda i: (i, 0)),
                  pl.BlockSpec((1, d), lambda i: (0, 0)),
                  pl.BlockSpec((None, 1, 3 * d), lambda i: (row_of_tile(i), 0, 0))],
        out_specs=pl.BlockSpec((tm, d), lambda i: (i, 0)),
        compiler_params=_cparams("parallel"),
        name="merge_out",
    )(u, wo, x2, g_post.reshape(1, d), mod3)


def _rope_tables(seq):
    rows = seq // GRID_W
    r = jnp.repeat(jnp.arange(rows, dtype=F32), GRID_W)
    col = jnp.tile(jnp.arange(GRID_W, dtype=F32), rows)
    n_freq = HEAD_DIM // 4
    inv = ROPE_THETA ** (-jnp.arange(n_freq, dtype=F32) / n_freq)
    ang = jnp.concatenate([r[:, None] * inv, col[:, None] * inv], axis=-1)
    cos, sin = jnp.cos(ang), jnp.sin(ang)
    return jnp.concatenate([cos, cos], axis=-1), jnp.concatenate([-sin, sin], axis=-1)


def _tile_types(d, tn, rope):
    cols = _col_offsets(d)
    r = 1 if rope else 0
    kinds = {"k_a": r, "k_b": r, "q_a": r | 2 | 4, "q_b": r | 2 | 4}
    out = [0] * (cols["in_cols"] // tn)
    for name, ty in kinds.items():
        width = d // A_GROUP if name == "k_a" else d
        assert cols[name] % tn == 0 and width % tn == 0
        for t in range(cols[name] // tn, (cols[name] + width) // tn):
            out[t] = ty
    return jnp.asarray(out, jnp.int32)


def kernel(x, c, ctx, c_ctx, w_ada, b_ada, g_pre, g_post, w_in, sink, lam_qk, g_subln,
           w_proj_a, w_proj_b, w_out):
    batch, seq, d = x.shape
    ctx_len = ctx.shape[1]
    depth = w_in.shape[0]
    t_lat, t_ctx = batch * seq, batch * ctx_len
    cols = _col_offsets(d)
    kv_cols, in_cols = cols["kv_cols"], cols["in_cols"]

    tn_in = min(512, d // A_GROUP)
    tm_lat = _pick(seq, (1024, 512, 256))
    tm_in = _pick(seq, (2048, 1024, 512, 256))
    tm_ctx = _pick(t_ctx, (512, 256))
    tq_a = _pick(seq, (1024, 512, 256, 128))
    tq_b = _pick(seq, (1024, 512, 256))
    tk_b = _pick(seq // 4, (512, 256))
    tn_m = _pick(d // 2, (1024, 512))
    tm_m = _pick(seq, (512, 256))

    cosf, sinf = _rope_tables(seq)
    zeros_tab = jnp.zeros((tm_ctx, HEAD_DIM), F32)
    types_lat = _tile_types(d, tn_in, rope=True)
    types_ctx = _tile_types(d, tn_in, rope=False)
    cvec8 = jnp.concatenate([c, c_ctx[None, :], jnp.zeros((8 - batch - 1, d), F32)], axis=0)

    x2 = x.reshape(t_lat, d)
    c2 = ctx.reshape(t_ctx, d)
    lat_row = lambda tm: (lambda i: i // (seq // tm))
    ctx_row = lambda i: batch

    for l in range(depth):
        last = l == depth - 1
        lam_init = 0.8 - 0.6 * math.exp(-0.3 * l)
        wpa, wpb, wo = w_proj_a[l].astype(BF16), w_proj_b[l].astype(BF16), w_out[l].astype(BF16)
        mod3 = _ada(cvec8, w_ada, b_ada, l).reshape(8, 1, 3 * d)

        hx = _prenorm(x2, g_pre[l], mod3, lat_row(tm_lat), tm_lat)
        hc = _prenorm(c2, g_pre[l], mod3, ctx_row, tm_ctx)
        pxl = _inproj(hx, w_in, l, types_lat, cosf, sinf, ncols=in_cols,
                      tm=tm_in, tn=tn_in, pos_tiles=seq // tm_in)
        pxc = _inproj(hc, w_in, l, types_ctx, zeros_tab, zeros_tab, ncols=kv_cols if last else in_cols,
                      tm=tm_ctx, tn=tn_in, pos_tiles=1)

        o_a = _attn_a(pxl, pxc, sink[l], batch=batch, seq=seq, ctx_len=ctx_len, d=d, tq=tq_a)
        o_b = _attn_b_latent(pxl, pxc, lam_qk[l], g_subln[l], lam_init=lam_init, batch=batch,
                             seq=seq, ctx_len=ctx_len, d=d, tq=tq_b, tk=tk_b)
        u = _merge1(o_a, o_b, pxl, wpa, wpb, d=d, tm=tm_m, tn=tn_m)

        if not last:
            oc_a = _attn_a_ctx(pxc, sink[l], batch=batch, ctx_len=ctx_len, d=d)
            oc_b = _attn_b_ctx(pxc, lam_qk[l], g_subln[l], lam_init=lam_init, batch=batch,
                               ctx_len=ctx_len, d=d)
            uc = _merge1(oc_a, oc_b, pxc, wpa, wpb, d=d, tm=tm_ctx, tn=tn_m)
            c2 = _merge2(uc, wo, c2, g_post[l], mod3, ctx_row, tm_ctx)

        x2 = _merge2(u, wo, x2, g_post[l], mod3, lat_row(tm_m), tm_m)

    return x2.reshape(batch, seq, d)
```

```python
import functools
import math

import jax
import jax.numpy as jnp
from jax import lax
from jax.experimental import pallas as pl
from jax.experimental.pallas import tpu as pltpu

HEAD_DIM = 128
GRID_W = 64
WINDOW = 128
A_GROUP = 4
ROPE_THETA = 10000.0
EPS = 1e-6
NEG = -1e30
Q_SCALE = HEAD_DIM ** -0.5
LOG2E = math.log2(math.e)
VMEM_LIMIT_BYTES = 56 * 1024 * 1024

F32 = jnp.float32
BF16 = jnp.bfloat16


def _cparams(*semantics):
    return pltpu.CompilerParams(dimension_semantics=semantics, vmem_limit_bytes=VMEM_LIMIT_BYTES)


def _pick(n, candidates):
    for c in candidates:
        if n % c == 0:
            return c
    raise ValueError(f"no tile in {candidates} divides {n}")


def _sigmoid(v):
    return 0.5 * jnp.tanh(0.5 * v) + 0.5


def _silu(v):
    h = 0.5 * v
    return h * jnp.tanh(h) + h


def _gate_out(o, z, dtype):
    return (o * _silu(z)).astype(dtype)


def _col_offsets(d):
    kv = d // A_GROUP
    names = ("k_a", "v_a", "k_b", "v_b", "q_a", "z_a", "q_b", "z_b", "g_a", "g_b")
    widths = (kv, kv, d, d, d, d, d, d, d, d)
    off, out = 0, {}
    for n, w in zip(names, widths):
        out[n] = off
        off += w
    out["kv_cols"] = out["q_a"]
    out["in_cols"] = off
    return out


def _ada_kernel(c_ref, w_ref, b_ref, o_ref):
    a = _silu(c_ref[...])
    o_ref[...] = jnp.dot(a, w_ref[...], preferred_element_type=F32,
                         precision=lax.Precision.HIGHEST) + b_ref[...]


def _ada(cvec8, w_all, b_all, layer):
    depth, d, n = w_all.shape
    tn = _pick(n, (1024, 768, 512, 384, 256, 128))
    return pl.pallas_call(
        _ada_kernel,
        out_shape=jax.ShapeDtypeStruct((8, n), F32),
        grid=(n // tn,),
        in_specs=[pl.BlockSpec((8, d), lambda j: (0, 0)),
                  pl.BlockSpec((None, d, tn), lambda j: (layer, 0, j)),
                  pl.BlockSpec((None, 1, tn), lambda j: (layer, 0, j))],
        out_specs=pl.BlockSpec((8, tn), lambda j: (0, j)),
        compiler_params=_cparams("arbitrary"),
        name="ada_ln",
    )(cvec8, w_all, b_all.reshape(depth, 1, n))


def _prenorm_kernel(x_ref, g_ref, mod_ref, o_ref, *, d):
    x = x_ref[...]
    y = x * lax.rsqrt(jnp.mean(x * x, axis=-1, keepdims=True) + EPS)
    y = y * g_ref[...]
    shift = mod_ref[:, 0:d]
    scale = mod_ref[:, d:2 * d]
    o_ref[...] = (y * (1.0 + scale) + shift).astype(o_ref.dtype)


def _prenorm(x2, g, mod3, row_of_tile, tm):
    t, d = x2.shape
    return pl.pallas_call(
        functools.partial(_prenorm_kernel, d=d),
        out_shape=jax.ShapeDtypeStruct((t, d), BF16),
        grid=(t // tm,),
        in_specs=[pl.BlockSpec((tm, d), lambda i: (i, 0)),
                  pl.BlockSpec((1, d), lambda i: (0, 0)),
                  pl.BlockSpec((None, 1, 3 * d), lambda i: (row_of_tile(i), 0, 0))],
        out_specs=pl.BlockSpec((tm, d), lambda i: (i, 0)),
        compiler_params=_cparams("parallel"),
        name="prenorm",
    )(x2, g.reshape(1, d), mod3)


def _inproj_kernel(type_ref, h_ref, w_ref, cos_ref, sin_ref, o_ref):
    t = type_ref[pl.program_id(1)]
    heads = o_ref.shape[1] // HEAD_DIM

    def matmul():
        return jnp.dot(h_ref[...], w_ref[...].astype(h_ref.dtype), preferred_element_type=F32)

    @pl.when(t == 0)
    def _():
        o_ref[...] = matmul().astype(o_ref.dtype)

    @pl.when(t != 0)
    def _():
        acc = matmul()
        rope = (t & 1).astype(F32)
        scale = (jnp.where((t & 2) != 0, Q_SCALE, 1.0) * jnp.where((t & 4) != 0, LOG2E, 1.0)).astype(F32)
        ca = (rope * cos_ref[...] + (1.0 - rope)) * scale
        sa = (rope * sin_ref[...]) * scale
        for hh in range(heads):
            a = acc[:, hh * HEAD_DIM:(hh + 1) * HEAD_DIM]
            r = a * ca + pltpu.roll(a, HEAD_DIM // 2, 1) * sa
            o_ref[:, hh * HEAD_DIM:(hh + 1) * HEAD_DIM] = r.astype(o_ref.dtype)


def _inproj(h, w_all, layer, types, cosf, sinf, *, ncols, tm, tn, pos_tiles):
    t, d = h.shape
    grid_spec = pltpu.PrefetchScalarGridSpec(
        num_scalar_prefetch=1,
        grid=(t // tm, ncols // tn),
        in_specs=[pl.BlockSpec((tm, d), lambda i, j, ty: (i, 0)),
                  pl.BlockSpec((None, d, tn), lambda i, j, ty: (layer, 0, j)),
                  pl.BlockSpec((tm, HEAD_DIM), lambda i, j, ty: (i % pos_tiles, 0)),
                  pl.BlockSpec((tm, HEAD_DIM), lambda i, j, ty: (i % pos_tiles, 0))],
        out_specs=pl.BlockSpec((tm, tn), lambda i, j, ty: (i, j)),
    )
    return pl.pallas_call(
        _inproj_kernel,
        out_shape=jax.ShapeDtypeStruct((t, ncols), BF16),
        grid_spec=grid_spec,
        compiler_params=_cparams("parallel", "arbitrary"),
        name="in_proj",
    )(types, h, w_all, cosf, sinf)


def _stack_heads(q):
    g = q.shape[1] // HEAD_DIM
    return jnp.concatenate([q[:, i * HEAD_DIM:(i + 1) * HEAD_DIM] for i in range(g)], axis=0)


def _unstack_heads(o, g):
    rows = o.shape[0] // g
    return jnp.concatenate([o[i * rows:(i + 1) * rows, :] for i in range(g)], axis=1)


def _sink_column(sink_ref, kvh, rows):
    return jnp.concatenate(
        [jnp.full((rows, 1), sink_ref[kvh * A_GROUP + g], F32) for g in range(A_GROUP)], axis=0)


def _qkt(q, k):
    return lax.dot_general(q, k, (((1,), (1,)), ((), ())), preferred_element_type=F32)


def _attn_a_kernel(sink_ref, q_ref, kp_ref, km_ref, kn_ref, vp_ref, vm_ref, vn_ref, kc_ref, vc_ref,
                   z_ref, o_ref, *, seq):
    i = pl.program_id(1)
    kvh = pl.program_id(2)
    tq = q_ref.shape[0]
    k_win = jnp.concatenate([kp_ref[...], km_ref[...], kn_ref[...]], axis=0)
    v_win = jnp.concatenate([vp_ref[...], vm_ref[...], vn_ref[...]], axis=0)
    kc = kc_ref[...]
    vc = vc_ref[...]
    rows = A_GROUP * WINDOW
    nloc = 3 * WINDOW
    nkeys = nloc + kc.shape[0]
    sink = _sink_column(sink_ref, kvh, WINDOW) * LOG2E
    qi = lax.broadcasted_iota(jnp.int32, (rows, nkeys), 0) & (WINDOW - 1)
    kj = lax.broadcasted_iota(jnp.int32, (rows, nkeys), 1)
    band = jnp.abs(qi + WINDOW - kj) <= WINDOW
    is_ctx = kj >= nloc
    nsub = tq // WINDOW

    def scores(r):
        q = _stack_heads(q_ref[r * WINDOW:(r + 1) * WINDOW, :])
        k = jnp.concatenate([k_win[r * WINDOW:(r + 3) * WINDOW, :], kc], axis=0)
        base = i * tq + (r - 1) * WINDOW
        valid = is_ctx | (band & (kj >= -base) & (kj < seq - base))
        return jnp.where(valid, _qkt(q, k), NEG)

    def lane_bcast(col):
        return jnp.broadcast_to(col, (rows, HEAD_DIM))

    sink_b = lane_bcast(sink)

    def softmax(s):
        groups = [s[:, g * HEAD_DIM:(g + 1) * HEAD_DIM] for g in range(nkeys // HEAD_DIM)]
        gmax = functools.reduce(jnp.maximum, groups)
        m = jnp.maximum(lane_bcast(jnp.max(gmax, axis=-1, keepdims=True)), sink_b)
        ps = [jnp.exp2(sg - m) for sg in groups]
        psum = functools.reduce(jnp.add, ps)
        denom = lane_bcast(jnp.sum(psum, axis=-1, keepdims=True)) + jnp.exp2(sink_b - m)
        return jnp.concatenate([pg.astype(vc.dtype) for pg in ps], axis=1), denom

    def weigh(r, p, denom):
        v = jnp.concatenate([v_win[r * WINDOW:(r + 3) * WINDOW, :], vc], axis=0)
        o = _unstack_heads(jnp.dot(p, v, preferred_element_type=F32) / denom, A_GROUP)
        z = z_ref[r * WINDOW:(r + 1) * WINDOW, :].astype(F32)
        o_ref[r * WINDOW:(r + 1) * WINDOW, :] = _gate_out(o, z, o_ref.dtype)

    s_next = scores(0)
    pd_next = None
    for r in range(nsub + 2):
        pd_cur, pd_next = pd_next, None
        s_cur, s_next = s_next, None
        if r < nsub - 1:
            s_next = scores(r + 1)
        if s_cur is not None and r < nsub:
            pd_next = softmax(s_cur)
        if pd_cur is not None:
            weigh(r - 1, *pd_cur)


def _attn_a(pxl, pxc, sink, *, batch, seq, ctx_len, d, tq):
    kvh = d // HEAD_DIM // A_GROUP
    gw = A_GROUP * HEAD_DIM
    nq = seq // tq
    r = tq // WINDOW
    nblk = seq // WINDOW
    cols = _col_offsets(d)
    assert cols["q_a"] % gw == 0
    assert cols["z_a"] % gw == 0
    qa, za = cols["q_a"] // gw, cols["z_a"] // gw
    ka_l = ka_c = cols["k_a"] // HEAD_DIM
    va_l = va_c = cols["v_a"] // HEAD_DIM

    def prev(b, i, h):
        return (b * nblk + jnp.maximum(i * r - 1, 0), h)

    def nxt(b, i, h):
        return (b * nblk + jnp.minimum((i + 1) * r, nblk - 1), h)

    halo = (WINDOW, HEAD_DIM)
    return pl.pallas_call(
        functools.partial(_attn_a_kernel, seq=seq),
        out_shape=jax.ShapeDtypeStruct((batch * seq, d), BF16),
        grid=(batch, nq, kvh),
        in_specs=[
            pl.BlockSpec(memory_space=pltpu.SMEM),
            pl.BlockSpec((tq, gw), lambda b, i, h: (b * nq + i, qa + h)),
            pl.BlockSpec(halo, lambda b, i, h: prev(b, i, ka_l + h)),
            pl.BlockSpec((tq, HEAD_DIM), lambda b, i, h: (b * nq + i, ka_l + h)),
            pl.BlockSpec(halo, lambda b, i, h: nxt(b, i, ka_l + h)),
            pl.BlockSpec(halo, lambda b, i, h: prev(b, i, va_l + h)),
            pl.BlockSpec((tq, HEAD_DIM), lambda b, i, h: (b * nq + i, va_l + h)),
            pl.BlockSpec(halo, lambda b, i, h: nxt(b, i, va_l + h)),
            pl.BlockSpec((ctx_len, HEAD_DIM), lambda b, i, h: (b, ka_c + h)),
            pl.BlockSpec((ctx_len, HEAD_DIM), lambda b, i, h: (b, va_c + h)),
            pl.BlockSpec((tq, gw), lambda b, i, h: (b * nq + i, za + h)),
        ],
        out_specs=pl.BlockSpec((tq, gw), lambda b, i, h: (b * nq + i, h)),
        compiler_params=_cparams("parallel", "parallel", "arbitrary"),
        name="attn_a_latent",
    )(sink, pxl, pxl, pxl, pxl, pxl, pxl, pxl, pxc, pxc, pxl)


def _attn_a_ctx_kernel(sink_ref, q_ref, k_ref, v_ref, z_ref, o_ref):
    kvh = pl.program_id(1)
    rows = q_ref.shape[0]
    q = _stack_heads(q_ref[...])
    v = v_ref[...]
    sink = _sink_column(sink_ref, kvh, rows) * LOG2E
    s = _qkt(q, k_ref[...])
    m = jnp.maximum(jnp.max(s, axis=-1, keepdims=True), sink)
    p = jnp.exp2(s - m)
    denom = jnp.sum(p, axis=-1, keepdims=True) + jnp.exp2(sink - m)
    o = _unstack_heads(jnp.dot(p.astype(v.dtype), v, preferred_element_type=F32) / denom, A_GROUP)
    o_ref[...] = _gate_out(o, z_ref[...].astype(F32), o_ref.dtype)


def _attn_a_ctx(pxc, sink, *, batch, ctx_len, d):
    kvh = d // HEAD_DIM // A_GROUP
    gw = A_GROUP * HEAD_DIM
    cols = _col_offsets(d)
    qa, za = cols["q_a"] // gw, cols["z_a"] // gw
    ka_c, va_c = cols["k_a"] // HEAD_DIM, cols["v_a"] // HEAD_DIM
    return pl.pallas_call(
        _attn_a_ctx_kernel,
        out_shape=jax.ShapeDtypeStruct((batch * ctx_len, d), BF16),
        grid=(batch, kvh),
        in_specs=[pl.BlockSpec(memory_space=pltpu.SMEM),
                  pl.BlockSpec((ctx_len, gw), lambda b, h: (b, qa + h)),
                  pl.BlockSpec((ctx_len, HEAD_DIM), lambda b, h: (b, ka_c + h)),
                  pl.BlockSpec((ctx_len, HEAD_DIM), lambda b, h: (b, va_c + h)),
                  pl.BlockSpec((ctx_len, gw), lambda b, h: (b, za + h))],
        out_specs=pl.BlockSpec((ctx_len, gw), lambda b, h: (b, h)),
        compiler_params=_cparams("parallel", "arbitrary"),
        name="attn_a_context",
    )(sink, pxc, pxc, pxc, pxc)


def _row_sum_lanes(x):
    groups = [x[:, g * HEAD_DIM:(g + 1) * HEAD_DIM] for g in range(x.shape[1] // HEAD_DIM)]
    part = functools.reduce(jnp.add, groups)
    return jnp.broadcast_to(jnp.sum(part, axis=-1, keepdims=True), part.shape)


def _diff_finish(o0, l0, o1, l1, lamqk_ref, g_ref, lam_init):
    lq = lamqk_ref[...]
    lam = (jnp.exp(jnp.sum(lq[0:1] * lq[1:2], axis=-1, keepdims=True))
           - jnp.exp(jnp.sum(lq[2:3] * lq[3:4], axis=-1, keepdims=True)) + lam_init)
    wide = lambda t: jnp.concatenate([t] * (o0.shape[1] // HEAD_DIM), axis=1)
    o = o0 / wide(l0) - lam * (o1 / wide(l1))
    inv = lax.rsqrt(_row_sum_lanes(o * o) * (1.0 / o.shape[1]) + EPS)
    return o * wide(inv) * g_ref[...] * (1.0 - lam_init)


def _attn_b_ctx_kernel(q_ref, k_ref, v_ref, lamqk_ref, g_ref, z_ref, o_ref, *, lam_init):
    v = v_ref[...]
    parts = []
    for mp in range(2):
        cols = slice(mp * HEAD_DIM, (mp + 1) * HEAD_DIM)
        s = _qkt(q_ref[:, cols], k_ref[:, cols])
        p = jnp.exp2(s - jnp.max(s, axis=-1, keepdims=True))
        parts += [jnp.dot(p.astype(v.dtype), v, preferred_element_type=F32), _row_sum_lanes(p)]
    o = _diff_finish(*parts, lamqk_ref, g_ref, lam_init)
    o_ref[...] = _gate_out(o, z_ref[...].astype(F32), o_ref.dtype)


def _attn_b_ctx(pxc, lamqk, g_sub, *, lam_init, batch, ctx_len, d):
    hw = 2 * HEAD_DIM
    heads = d // hw
    cols = _col_offsets(d)
    qb, kb, vb, zb = cols["q_b"] // hw, cols["k_b"] // hw, cols["v_b"] // hw, cols["z_b"] // hw
    return pl.pallas_call(
        functools.partial(_attn_b_ctx_kernel, lam_init=lam_init),
        out_shape=jax.ShapeDtypeStruct((batch * ctx_len, d), BF16),
        grid=(batch, heads),
        in_specs=[pl.BlockSpec((ctx_len, hw), lambda b, h: (b, qb + h)),
                  pl.BlockSpec((ctx_len, hw), lambda b, h: (b, kb + h)),
                  pl.BlockSpec((ctx_len, hw), lambda b, h: (b, vb + h)),
                  pl.BlockSpec((4, HEAD_DIM), lambda b, h: (0, 0)),
                  pl.BlockSpec((1, hw), lambda b, h: (0, 0)),
                  pl.BlockSpec((ctx_len, hw), lambda b, h: (b, zb + h))],
        out_specs=pl.BlockSpec((ctx_len, hw), lambda b, h: (b, h)),
        compiler_params=_cparams("parallel", "arbitrary"),
        name="attn_b_context",
    )(pxc, pxc, pxc, lamqk, g_sub.reshape(1, hw), pxc)


def _attn_b_latent_kernel(q_ref, k_ref, v_ref, kc_ref, vc_ref, lamqk_ref, g_ref, z_ref, o_ref,
                          sa_scr, sb_scr, xa_scr, xb_scr, pa_scr, pb_scr, ala_scr, alb_scr,
                          m_scr, l_scr, acc_scr, *, lam_init, tk):
    nk = k_ref.shape[0] // tk
    tq = q_ref.shape[0]
    lanes = HEAD_DIM

    def lane_groups(x):
        return [x[:, g * lanes:(g + 1) * lanes] for g in range(x.shape[1] // lanes)]

    def group_max(s):
        return functools.reduce(jnp.maximum, lane_groups(s))

    def row_max(gmax):
        return jnp.broadcast_to(jnp.max(gmax, axis=-1, keepdims=True), (tq, lanes))

    nctx = kc_ref.shape[0]

    def scores(c, buf, with_ctx=False):
        s_scr, x_scr = buf[0], buf[1]
        start = pl.multiple_of(c * tk, tk)
        for mp in range(2):
            cols = slice(mp * HEAD_DIM, (mp + 1) * HEAD_DIM)
            s = _qkt(q_ref[:, cols], k_ref[pl.ds(start, tk), cols])
            s_scr[mp, :, 0:tk] = s
            gmax = group_max(s)
            if with_ctx:
                sc = _qkt(q_ref[:, cols], kc_ref[:, cols])
                s_scr[mp, :, tk:tk + nctx] = sc
                gmax = jnp.maximum(gmax, group_max(sc))
            x_scr[mp] = gmax

    def softmax(buf, first=False):
        s_scr, x_scr, p_scr, al_scr = buf
        width = tk + nctx if first else tk
        for mp in range(2):
            m_new = row_max(x_scr[mp])
            if not first:
                m_old = m_scr[mp]
                m_new = jnp.maximum(m_old, m_new)
                alpha = jnp.exp2(m_old - m_new)
            psum = None
            for g in range(width // lanes):
                cols = slice(g * lanes, (g + 1) * lanes)
                pg = jnp.exp2(s_scr[mp, :, cols] - m_new)
                p_scr[mp, :, cols] = pg.astype(p_scr.dtype)
                psum = pg if psum is None else psum + pg
            if first:
                l_scr[mp] = psum
            else:
                l_scr[mp] = alpha * l_scr[mp] + psum
                al_scr[mp] = alpha
            m_scr[mp] = m_new

    def weigh(c, buf, first=False):
        p_scr, al_scr = buf[2], buf[3]
        v = v_ref[pl.ds(pl.multiple_of(c * tk, tk), tk), :]
        for mp in range(2):
            pv = jnp.dot(p_scr[mp, :, 0:tk], v, preferred_element_type=F32)
            if first:
                acc_scr[mp] = pv + jnp.dot(p_scr[mp, :, tk:tk + nctx], vc_ref[...],
                                           preferred_element_type=F32)
            else:
                al = jnp.concatenate([al_scr[mp]] * (acc_scr.shape[2] // lanes), axis=1)
                acc_scr[mp] = al * acc_scr[mp] + pv

    buf_a = (sa_scr, xa_scr, pa_scr, ala_scr)
    buf_b = (sb_scr, xb_scr, pb_scr, alb_scr)
    bufs = (buf_a, buf_b)

    def steps(c0, n, parity):
        for t in range(n):
            cur, nxt = bufs[(parity + t) % 2], bufs[(parity + t + 1) % 2]
            weigh(c0 + t, cur)
            softmax(nxt)
            scores(c0 + t + 2, cur)

    scores(0, buf_a, with_ctx=True)
    softmax(buf_a, first=True)
    scores(1, buf_b)
    weigh(0, buf_a, first=True)
    softmax(buf_b)
    scores(2, buf_a)

    unroll = 4
    trips = (nk - 3) // unroll

    def body(i, carry):
        steps(1 + i * unroll, unroll, 1)
        return carry

    lax.fori_loop(0, trips, body, 0)
    steps(1 + trips * unroll, nk - 3 - trips * unroll, 1)
    weigh(nk - 2, buf_a)
    softmax(buf_b)
    weigh(nk - 1, buf_b)

    o = _diff_finish(acc_scr[0], _row_sum_lanes(l_scr[0]), acc_scr[1], _row_sum_lanes(l_scr[1]),
                     lamqk_ref, g_ref, lam_init)
    o_ref[...] = _gate_out(o, z_ref[...].astype(F32), o_ref.dtype)


def _attn_b_latent(pxl, pxc, lamqk, g_sub, *, lam_init, batch, seq, ctx_len, d, tq, tk):
    hw = 2 * HEAD_DIM
    heads = d // hw
    nq = seq // tq
    assert (seq // tk) % 2 == 0 and seq // tk >= 4 and ctx_len % HEAD_DIM == 0
    cols = _col_offsets(d)
    qb, zb = cols["q_b"] // hw, cols["z_b"] // hw
    kb_l = kb_c = cols["k_b"] // hw
    vb_l = vb_c = cols["v_b"] // hw
    return pl.pallas_call(
        functools.partial(_attn_b_latent_kernel, lam_init=lam_init, tk=tk),
        out_shape=jax.ShapeDtypeStruct((batch * seq, d), BF16),
        grid=(batch, heads, nq),
        in_specs=[pl.BlockSpec((tq, hw), lambda b, h, i: (b * nq + i, qb + h)),
                  pl.BlockSpec((seq, hw), lambda b, h, i: (b, kb_l + h)),
                  pl.BlockSpec((seq, hw), lambda b, h, i: (b, vb_l + h)),
                  pl.BlockSpec((ctx_len, hw), lambda b, h, i: (b, kb_c + h)),
                  pl.BlockSpec((ctx_len, hw), lambda b, h, i: (b, vb_c + h)),
                  pl.BlockSpec((4, HEAD_DIM), lambda b, h, i: (0, 0)),
                  pl.BlockSpec((1, hw), lambda b, h, i: (0, 0)),
                  pl.BlockSpec((tq, hw), lambda b, h, i: (b * nq + i, zb + h))],
        out_specs=pl.BlockSpec((tq, hw), lambda b, h, i: (b * nq + i, h)),
        scratch_shapes=[pltpu.VMEM((2, tq, tk + ctx_len), F32), pltpu.VMEM((2, tq, tk), F32),
                        pltpu.VMEM((2, tq, HEAD_DIM), F32), pltpu.VMEM((2, tq, HEAD_DIM), F32),
                        pltpu.VMEM((2, tq, tk + ctx_len), BF16), pltpu.VMEM((2, tq, tk), BF16),
                        pltpu.VMEM((2, tq, HEAD_DIM), F32), pltpu.VMEM((2, tq, HEAD_DIM), F32),
                        pltpu.VMEM((2, tq, HEAD_DIM), F32), pltpu.VMEM((2, tq, HEAD_DIM), F32),
                        pltpu.VMEM((2, tq, hw), F32)],
        compiler_params=_cparams("parallel", "parallel", "arbitrary"),
        name="attn_b_latent",
    )(pxl, pxl, pxl, pxc, pxc, lamqk, g_sub.reshape(1, hw), pxl)


def _merge1_kernel(a_ref, b_ref, ga0_ref, ga1_ref, gb0_ref, gb1_ref, wa_ref, wb_ref, u_ref):
    half = ga0_ref.shape[1]
    ya = jnp.dot(a_ref[...], wa_ref[...], preferred_element_type=F32)
    yb = jnp.dot(b_ref[...], wb_ref[...], preferred_element_type=F32)
    for part, (ga_ref, gb_ref) in enumerate(((ga0_ref, gb0_ref), (ga1_ref, gb1_ref))):
        cols = slice(part * half, (part + 1) * half)
        u = (_sigmoid(ga_ref[...].astype(F32)) * ya[:, cols]
             + _sigmoid(gb_ref[...].astype(F32)) * yb[:, cols])
        u_ref[:, cols] = u.astype(u_ref.dtype)


def _merge1(a, b, px, wpa, wpb, *, d, tm):
    t = a.shape[0]
    half = d // 2
    cols = _col_offsets(d)
    assert cols["g_a"] % half == 0 and cols["g_b"] % half == 0
    ga, gb = cols["g_a"] // half, cols["g_b"] // half
    resident = pl.Buffered(1)
    return pl.pallas_call(
        _merge1_kernel,
        out_shape=jax.ShapeDtypeStruct((t, d), BF16),
        grid=(t // tm,),
        in_specs=[pl.BlockSpec((tm, d), lambda i: (i, 0)),
                  pl.BlockSpec((tm, d), lambda i: (i, 0)),
                  pl.BlockSpec((tm, half), lambda i: (i, ga)),
                  pl.BlockSpec((tm, half), lambda i: (i, ga + 1)),
                  pl.BlockSpec((tm, half), lambda i: (i, gb)),
                  pl.BlockSpec((tm, half), lambda i: (i, gb + 1)),
                  pl.BlockSpec((d, d), lambda i: (0, 0), pipeline_mode=resident),
                  pl.BlockSpec((d, d), lambda i: (0, 0), pipeline_mode=resident)],
        out_specs=pl.BlockSpec((tm, d), lambda i: (i, 0)),
        compiler_params=_cparams("parallel"),
        name="merge_gate",
    )(a, b, px, px, px, px, wpa, wpb)


def _merge2_kernel(u_ref, wo_ref, x_ref, g_ref, mod_ref, o_ref, *, d):
    y = jnp.dot(u_ref[...], wo_ref[...], preferred_element_type=F32)
    n = y * lax.rsqrt(jnp.mean(y * y, axis=-1, keepdims=True) + EPS) * g_ref[...]
    o_ref[...] = x_ref[...] + mod_ref[:, 2 * d:3 * d] * n


def _merge2(u, wo, x2, g_post, mod3, row_of_tile, tm):
    t, d = x2.shape
    return pl.pallas_call(
        functools.partial(_merge2_kernel, d=d),
        out_shape=jax.ShapeDtypeStruct((t, d), F32),
        grid=(t // tm,),
        in_specs=[pl.BlockSpec((tm, d), lambda i: (i, 0)),
                  pl.BlockSpec((d, d), lambda i: (0, 0)),
                  pl.BlockSpec((tm, d), lambda i: (i, 0)),
                  pl.BlockSpec((1, d), lambda i: (0, 0)),
                  pl.BlockSpec((None, 1, 3 * d), lambda i: (row_of_tile(i), 0, 0))],
        out_specs=pl.BlockSpec((tm, d), lambda i: (i, 0)),
        compiler_params=_cparams("parallel"),
        name="merge_out",
    )(u, wo, x2, g_post.reshape(1, d), mod3)


def _rope_tables(seq):
    rows = seq // GRID_W
    r = jnp.repeat(jnp.arange(rows, dtype=F32), GRID_W)
    col = jnp.tile(jnp.arange(GRID_W, dtype=F32), rows)
    n_freq = HEAD_DIM // 4
    inv = ROPE_THETA ** (-jnp.arange(n_freq, dtype=F32) / n_freq)
    ang = jnp.concatenate([r[:, None] * inv, col[:, None] * inv], axis=-1)
    cos, sin = jnp.cos(ang), jnp.sin(ang)
    return jnp.concatenate([cos, cos], axis=-1), jnp.concatenate([-sin, sin], axis=-1)


def _tile_types(d, tn, rope):
    cols = _col_offsets(d)
    r = 1 if rope else 0
    kinds = {"k_a": r, "k_b": r, "q_a": r | 2 | 4, "q_b": r | 2 | 4}
    out = [0] * (cols["in_cols"] // tn)
    for name, ty in kinds.items():
        width = d // A_GROUP if name == "k_a" else d
        assert cols[name] % tn == 0 and width % tn == 0
        for t in range(cols[name] // tn, (cols[name] + width) // tn):
            out[t] = ty
    return jnp.asarray(out, jnp.int32)


def kernel(x, c, ctx, c_ctx, w_ada, b_ada, g_pre, g_post, w_in, sink, lam_qk, g_subln,
           w_proj_a, w_proj_b, w_out):
    batch, seq, d = x.shape
    ctx_len = ctx.shape[1]
    depth = w_in.shape[0]
    t_lat, t_ctx = batch * seq, batch * ctx_len
    cols = _col_offsets(d)
    kv_cols, in_cols = cols["kv_cols"], cols["in_cols"]

    tn_in = min(512, d // A_GROUP)
    tm_lat = _pick(seq, (1024, 512, 256))
    tm_in = _pick(seq, (2048, 1024, 512, 256))
    tm_ctx = _pick(t_ctx, (512, 256))
    tq_a = _pick(seq, (1024, 512, 256, 128))
    tq_b = _pick(seq, (1024, 512, 256))
    tk_b = _pick(seq // 4, (512, 256))
    tm_m = _pick(seq, (512, 256))

    cosf, sinf = _rope_tables(seq)
    zeros_tab = jnp.zeros((tm_ctx, HEAD_DIM), F32)
    types_lat = _tile_types(d, tn_in, rope=True)
    types_ctx = _tile_types(d, tn_in, rope=False)
    cvec8 = jnp.concatenate([c, c_ctx[None, :], jnp.zeros((8 - batch - 1, d), F32)], axis=0)

    x2 = x.reshape(t_lat, d)
    c2 = ctx.reshape(t_ctx, d)
    lat_row = lambda tm: (lambda i: i // (seq // tm))
    ctx_row = lambda i: batch

    for l in range(depth):
        last = l == depth - 1
        lam_init = 0.8 - 0.6 * math.exp(-0.3 * l)
        wpa, wpb, wo = w_proj_a[l].astype(BF16), w_proj_b[l].astype(BF16), w_out[l].astype(BF16)
        mod3 = _ada(cvec8, w_ada, b_ada, l).reshape(8, 1, 3 * d)

        hx = _prenorm(x2, g_pre[l], mod3, lat_row(tm_lat), tm_lat)
        hc = _prenorm(c2, g_pre[l], mod3, ctx_row, tm_ctx)
        pxl = _inproj(hx, w_in, l, types_lat, cosf, sinf, ncols=in_cols,
                      tm=tm_in, tn=tn_in, pos_tiles=seq // tm_in)
        pxc = _inproj(hc, w_in, l, types_ctx, zeros_tab, zeros_tab, ncols=kv_cols if last else in_cols,
                      tm=tm_ctx, tn=tn_in, pos_tiles=1)

        o_a = _attn_a(pxl, pxc, sink[l], batch=batch, seq=seq, ctx_len=ctx_len, d=d, tq=tq_a)
        o_b = _attn_b_latent(pxl, pxc, lam_qk[l], g_subln[l], lam_init=lam_init, batch=batch,
                             seq=seq, ctx_len=ctx_len, d=d, tq=tq_b, tk=tk_b)
        u = _merge1(o_a, o_b, pxl, wpa, wpb, d=d, tm=tm_m)

        if not last:
            oc_a = _attn_a_ctx(pxc, sink[l], batch=batch, ctx_len=ctx_len, d=d)
            oc_b = _attn_b_ctx(pxc, lam_qk[l], g_subln[l], lam_init=lam_init, batch=batch,
                               ctx_len=ctx_len, d=d)
            uc = _merge1(oc_a, oc_b, pxc, wpa, wpb, d=d, tm=tm_ctx)
            c2 = _merge2(uc, wo, c2, g_post[l], mod3, ctx_row, tm_ctx)

        x2 = _merge2(u, wo, x2, g_post[l], mod3, lat_row(tm_m), tm_m)

    return x2.reshape(batch, seq, d)
```

```python
import functools
import math

import jax
import jax.numpy as jnp
from jax import lax
from jax.experimental import pallas as pl
from jax.experimental.pallas import tpu as pltpu

HEAD_DIM = 128
GRID_W = 64
WINDOW = 128
A_GROUP = 4
ROPE_THETA = 10000.0
EPS = 1e-6
NEG = -1e30
Q_SCALE = HEAD_DIM ** -0.5
LOG2E = math.log2(math.e)
VMEM_LIMIT_BYTES = 56 * 1024 * 1024

F32 = jnp.float32
BF16 = jnp.bfloat16


def _cparams(*semantics):
    return pltpu.CompilerParams(dimension_semantics=semantics, vmem_limit_bytes=VMEM_LIMIT_BYTES)


def _pick(n, candidates):
    for c in candidates:
        if n % c == 0:
            return c
    raise ValueError(f"no tile in {candidates} divides {n}")


def _sigmoid(v):
    return 0.5 * jnp.tanh(0.5 * v) + 0.5


def _silu(v):
    h = 0.5 * v
    return h * jnp.tanh(h) + h


def _gate_out(o, z, dtype):
    return (o * _silu(z)).astype(dtype)


def _col_offsets(d):
    kv = d // A_GROUP
    names = ("k_a", "v_a", "k_b", "v_b", "q_a", "z_a", "q_b", "z_b", "g_a", "g_b")
    widths = (kv, kv, d, d, d, d, d, d, d, d)
    off, out = 0, {}
    for n, w in zip(names, widths):
        out[n] = off
        off += w
    out["kv_cols"] = out["q_a"]
    out["in_cols"] = off
    return out


def _ada_kernel(c_ref, w_ref, b_ref, o_ref):
    a = _silu(c_ref[...])
    o_ref[...] = jnp.dot(a, w_ref[...], preferred_element_type=F32,
                         precision=lax.Precision.HIGHEST) + b_ref[...]


def _ada(cvec8, w_all, b_all, layer):
    depth, d, n = w_all.shape
    tn = _pick(n, (1024, 768, 512, 384, 256, 128))
    return pl.pallas_call(
        _ada_kernel,
        out_shape=jax.ShapeDtypeStruct((8, n), F32),
        grid=(n // tn,),
        in_specs=[pl.BlockSpec((8, d), lambda j: (0, 0)),
                  pl.BlockSpec((None, d, tn), lambda j: (layer, 0, j)),
                  pl.BlockSpec((None, 1, tn), lambda j: (layer, 0, j))],
        out_specs=pl.BlockSpec((8, tn), lambda j: (0, j)),
        compiler_params=_cparams("arbitrary"),
        name="ada_ln",
    )(cvec8, w_all, b_all.reshape(depth, 1, n))


def _prenorm_kernel(x_ref, g_ref, mod_ref, o_ref, *, d):
    x = x_ref[...]
    y = x * lax.rsqrt(jnp.mean(x * x, axis=-1, keepdims=True) + EPS)
    y = y * g_ref[...]
    shift = mod_ref[:, 0:d]
    scale = mod_ref[:, d:2 * d]
    o_ref[...] = (y * (1.0 + scale) + shift).astype(o_ref.dtype)


def _prenorm(x2, g, mod3, row_of_tile, tm):
    t, d = x2.shape
    return pl.pallas_call(
        functools.partial(_prenorm_kernel, d=d),
        out_shape=jax.ShapeDtypeStruct((t, d), BF16),
        grid=(t // tm,),
        in_specs=[pl.BlockSpec((tm, d), lambda i: (i, 0)),
                  pl.BlockSpec((1, d), lambda i: (0, 0)),
                  pl.BlockSpec((None, 1, 3 * d), lambda i: (row_of_tile(i), 0, 0))],
        out_specs=pl.BlockSpec((tm, d), lambda i: (i, 0)),
        compiler_params=_cparams("parallel"),
        name="prenorm",
    )(x2, g.reshape(1, d), mod3)


def _inproj_kernel(type_ref, h_ref, w_ref, cos_ref, sin_ref, o_ref):
    t = type_ref[pl.program_id(1)]
    heads = o_ref.shape[1] // HEAD_DIM

    def matmul():
        return jnp.dot(h_ref[...], w_ref[...].astype(h_ref.dtype), preferred_element_type=F32)

    @pl.when(t == 0)
    def _():
        o_ref[...] = matmul().astype(o_ref.dtype)

    @pl.when(t != 0)
    def _():
        acc = matmul()
        rope = (t & 1).astype(F32)
        scale = (jnp.where((t & 2) != 0, Q_SCALE, 1.0) * jnp.where((t & 4) != 0, LOG2E, 1.0)).astype(F32)
        ca = (rope * cos_ref[...] + (1.0 - rope)) * scale
        sa = (rope * sin_ref[...]) * scale
        for hh in range(heads):
            a = acc[:, hh * HEAD_DIM:(hh + 1) * HEAD_DIM]
            r = a * ca + pltpu.roll(a, HEAD_DIM // 2, 1) * sa
            o_ref[:, hh * HEAD_DIM:(hh + 1) * HEAD_DIM] = r.astype(o_ref.dtype)


def _inproj(h, w_all, layer, types, cosf, sinf, *, ncols, tm, tn, pos_tiles):
    t, d = h.shape
    grid_spec = pltpu.PrefetchScalarGridSpec(
        num_scalar_prefetch=1,
        grid=(t // tm, ncols // tn),
        in_specs=[pl.BlockSpec((tm, d), lambda i, j, ty: (i, 0)),
                  pl.BlockSpec((None, d, tn), lambda i, j, ty: (layer, 0, j)),
                  pl.BlockSpec((tm, HEAD_DIM), lambda i, j, ty: (i % pos_tiles, 0)),
                  pl.BlockSpec((tm, HEAD_DIM), lambda i, j, ty: (i % pos_tiles, 0))],
        out_specs=pl.BlockSpec((tm, tn), lambda i, j, ty: (i, j)),
    )
    return pl.pallas_call(
        _inproj_kernel,
        out_shape=jax.ShapeDtypeStruct((t, ncols), BF16),
        grid_spec=grid_spec,
        compiler_params=_cparams("parallel", "arbitrary"),
        name="in_proj",
    )(types, h, w_all, cosf, sinf)


def _stack_heads(q):
    g = q.shape[1] // HEAD_DIM
    return jnp.concatenate([q[:, i * HEAD_DIM:(i + 1) * HEAD_DIM] for i in range(g)], axis=0)


def _unstack_heads(o, g):
    rows = o.shape[0] // g
    return jnp.concatenate([o[i * rows:(i + 1) * rows, :] for i in range(g)], axis=1)


def _sink_column(sink_ref, kvh, rows):
    return jnp.concatenate(
        [jnp.full((rows, 1), sink_ref[kvh * A_GROUP + g], F32) for g in range(A_GROUP)], axis=0)


def _qkt(q, k):
    return lax.dot_general(q, k, (((1,), (1,)), ((), ())), preferred_element_type=F32)


def _attn_a_kernel(sink_ref, q_ref, kp_ref, km_ref, kn_ref, vp_ref, vm_ref, vn_ref, kc_ref, vc_ref,
                   z_ref, o_ref, *, seq):
    i = pl.program_id(1)
    kvh = pl.program_id(2)
    tq = q_ref.shape[0]
    k_win = jnp.concatenate([kp_ref[...], km_ref[...], kn_ref[...]], axis=0)
    v_win = jnp.concatenate([vp_ref[...], vm_ref[...], vn_ref[...]], axis=0)
    kc = kc_ref[...]
    vc = vc_ref[...]
    rows = A_GROUP * WINDOW
    nloc = 3 * WINDOW
    nkeys = nloc + kc.shape[0]
    sink = _sink_column(sink_ref, kvh, WINDOW) * LOG2E
    qi = lax.broadcasted_iota(jnp.int32, (rows, nkeys), 0) & (WINDOW - 1)
    kj = lax.broadcasted_iota(jnp.int32, (rows, nkeys), 1)
    band = jnp.abs(qi + WINDOW - kj) <= WINDOW
    is_ctx = kj >= nloc
    nsub = tq // WINDOW

    def scores(r):
        q = _stack_heads(q_ref[r * WINDOW:(r + 1) * WINDOW, :])
        k = jnp.concatenate([k_win[r * WINDOW:(r + 3) * WINDOW, :], kc], axis=0)
        base = i * tq + (r - 1) * WINDOW
        valid = is_ctx | (band & (kj >= -base) & (kj < seq - base))
        return jnp.where(valid, _qkt(q, k), NEG)

    def lane_bcast(col):
        return jnp.broadcast_to(col, (rows, HEAD_DIM))

    sink_b = lane_bcast(sink)

    def softmax(s):
        groups = [s[:, g * HEAD_DIM:(g + 1) * HEAD_DIM] for g in range(nkeys // HEAD_DIM)]
        gmax = functools.reduce(jnp.maximum, groups)
        m = jnp.maximum(lane_bcast(jnp.max(gmax, axis=-1, keepdims=True)), sink_b)
        ps = [jnp.exp2(sg - m) for sg in groups]
        psum = functools.reduce(jnp.add, ps)
        denom = lane_bcast(jnp.sum(psum, axis=-1, keepdims=True)) + jnp.exp2(sink_b - m)
        return jnp.concatenate([pg.astype(vc.dtype) for pg in ps], axis=1), denom

    def weigh(r, p, denom):
        v = jnp.concatenate([v_win[r * WINDOW:(r + 3) * WINDOW, :], vc], axis=0)
        o = _unstack_heads(jnp.dot(p, v, preferred_element_type=F32) / denom, A_GROUP)
        z = z_ref[r * WINDOW:(r + 1) * WINDOW, :].astype(F32)
        o_ref[r * WINDOW:(r + 1) * WINDOW, :] = _gate_out(o, z, o_ref.dtype)

    s_next = scores(0)
    pd_next = None
    for r in range(nsub + 2):
        pd_cur, pd_next = pd_next, None
        s_cur, s_next = s_next, None
        if r < nsub - 1:
            s_next = scores(r + 1)
        if s_cur is not None and r < nsub:
            pd_next = softmax(s_cur)
        if pd_cur is not None:
            weigh(r - 1, *pd_cur)


def _attn_a(pxl, pxc, sink, *, batch, seq, ctx_len, d, tq):
    kvh = d // HEAD_DIM // A_GROUP
    gw = A_GROUP * HEAD_DIM
    nq = seq // tq
    r = tq // WINDOW
    nblk = seq // WINDOW
    cols = _col_offsets(d)
    assert cols["q_a"] % gw == 0
    assert cols["z_a"] % gw == 0
    qa, za = cols["q_a"] // gw, cols["z_a"] // gw
    ka_l = ka_c = cols["k_a"] // HEAD_DIM
    va_l = va_c = cols["v_a"] // HEAD_DIM

    def prev(b, i, h):
        return (b * nblk + jnp.maximum(i * r - 1, 0), h)

    def nxt(b, i, h):
        return (b * nblk + jnp.minimum((i + 1) * r, nblk - 1), h)

    halo = (WINDOW, HEAD_DIM)
    return pl.pallas_call(
        functools.partial(_attn_a_kernel, seq=seq),
        out_shape=jax.ShapeDtypeStruct((batch * seq, d), BF16),
        grid=(batch, nq, kvh),
        in_specs=[
            pl.BlockSpec(memory_space=pltpu.SMEM),
            pl.BlockSpec((tq, gw), lambda b, i, h: (b * nq + i, qa + h)),
            pl.BlockSpec(halo, lambda b, i, h: prev(b, i, ka_l + h)),
            pl.BlockSpec((tq, HEAD_DIM), lambda b, i, h: (b * nq + i, ka_l + h)),
            pl.BlockSpec(halo, lambda b, i, h: nxt(b, i, ka_l + h)),
            pl.BlockSpec(halo, lambda b, i, h: prev(b, i, va_l + h)),
            pl.BlockSpec((tq, HEAD_DIM), lambda b, i, h: (b * nq + i, va_l + h)),
            pl.BlockSpec(halo, lambda b, i, h: nxt(b, i, va_l + h)),
            pl.BlockSpec((ctx_len, HEAD_DIM), lambda b, i, h: (b, ka_c + h)),
            pl.BlockSpec((ctx_len, HEAD_DIM), lambda b, i, h: (b, va_c + h)),
            pl.BlockSpec((tq, gw), lambda b, i, h: (b * nq + i, za + h)),
        ],
        out_specs=pl.BlockSpec((tq, gw), lambda b, i, h: (b * nq + i, h)),
        compiler_params=_cparams("parallel", "parallel", "arbitrary"),
        name="attn_a_latent",
    )(sink, pxl, pxl, pxl, pxl, pxl, pxl, pxl, pxc, pxc, pxl)


def _attn_a_ctx_kernel(sink_ref, q_ref, k_ref, v_ref, z_ref, o_ref):
    kvh = pl.program_id(1)
    rows = q_ref.shape[0]
    q = _stack_heads(q_ref[...])
    v = v_ref[...]
    sink = _sink_column(sink_ref, kvh, rows) * LOG2E
    s = _qkt(q, k_ref[...])
    m = jnp.maximum(jnp.max(s, axis=-1, keepdims=True), sink)
    p = jnp.exp2(s - m)
    denom = jnp.sum(p, axis=-1, keepdims=True) + jnp.exp2(sink - m)
    o = _unstack_heads(jnp.dot(p.astype(v.dtype), v, preferred_element_type=F32) / denom, A_GROUP)
    o_ref[...] = _gate_out(o, z_ref[...].astype(F32), o_ref.dtype)


def _attn_a_ctx(pxc, sink, *, batch, ctx_len, d):
    kvh = d // HEAD_DIM // A_GROUP
    gw = A_GROUP * HEAD_DIM
    cols = _col_offsets(d)
    qa, za = cols["q_a"] // gw, cols["z_a"] // gw
    ka_c, va_c = cols["k_a"] // HEAD_DIM, cols["v_a"] // HEAD_DIM
    return pl.pallas_call(
        _attn_a_ctx_kernel,
        out_shape=jax.ShapeDtypeStruct((batch * ctx_len, d), BF16),
        grid=(batch, kvh),
        in_specs=[pl.BlockSpec(memory_space=pltpu.SMEM),
                  pl.BlockSpec((ctx_len, gw), lambda b, h: (b, qa + h)),
                  pl.BlockSpec((ctx_len, HEAD_DIM), lambda b, h: (b, ka_c + h)),
                  pl.BlockSpec((ctx_len, HEAD_DIM), lambda b, h: (b, va_c + h)),
                  pl.BlockSpec((ctx_len, gw), lambda b, h: (b, za + h))],
        out_specs=pl.BlockSpec((ctx_len, gw), lambda b, h: (b, h)),
        compiler_params=_cparams("parallel", "arbitrary"),
        name="attn_a_context",
    )(sink, pxc, pxc, pxc, pxc)


def _row_sum_lanes(x):
    groups = [x[:, g * HEAD_DIM:(g + 1) * HEAD_DIM] for g in range(x.shape[1] // HEAD_DIM)]
    part = functools.reduce(jnp.add, groups)
    return jnp.broadcast_to(jnp.sum(part, axis=-1, keepdims=True), part.shape)


def _diff_finish(o0, l0, o1, l1, lamqk_ref, g_ref, lam_init):
    lq = lamqk_ref[...]
    lam = (jnp.exp(jnp.sum(lq[0:1] * lq[1:2], axis=-1, keepdims=True))
           - jnp.exp(jnp.sum(lq[2:3] * lq[3:4], axis=-1, keepdims=True)) + lam_init)
    wide = lambda t: jnp.concatenate([t] * (o0.shape[1] // HEAD_DIM), axis=1)
    o = o0 / wide(l0) - lam * (o1 / wide(l1))
    inv = lax.rsqrt(_row_sum_lanes(o * o) * (1.0 / o.shape[1]) + EPS)
    return o * wide(inv) * g_ref[...] * (1.0 - lam_init)


def _attn_b_ctx_kernel(q_ref, k_ref, v_ref, lamqk_ref, g_ref, z_ref, o_ref, *, lam_init):
    v = v_ref[...]
    parts = []
    for mp in range(2):
        cols = slice(mp * HEAD_DIM, (mp + 1) * HEAD_DIM)
        s = _qkt(q_ref[:, cols], k_ref[:, cols])
        p = jnp.exp2(s - jnp.max(s, axis=-1, keepdims=True))
        parts += [jnp.dot(p.astype(v.dtype), v, preferred_element_type=F32), _row_sum_lanes(p)]
    o = _diff_finish(*parts, lamqk_ref, g_ref, lam_init)
    o_ref[...] = _gate_out(o, z_ref[...].astype(F32), o_ref.dtype)


def _attn_b_ctx(pxc, lamqk, g_sub, *, lam_init, batch, ctx_len, d):
    hw = 2 * HEAD_DIM
    heads = d // hw
    cols = _col_offsets(d)
    qb, kb, vb, zb = cols["q_b"] // hw, cols["k_b"] // hw, cols["v_b"] // hw, cols["z_b"] // hw
    return pl.pallas_call(
        functools.partial(_attn_b_ctx_kernel, lam_init=lam_init),
        out_shape=jax.ShapeDtypeStruct((batch * ctx_len, d), BF16),
        grid=(batch, heads),
        in_specs=[pl.BlockSpec((ctx_len, hw), lambda b, h: (b, qb + h)),
                  pl.BlockSpec((ctx_len, hw), lambda b, h: (b, kb + h)),
                  pl.BlockSpec((ctx_len, hw), lambda b, h: (b, vb + h)),
                  pl.BlockSpec((4, HEAD_DIM), lambda b, h: (0, 0)),
                  pl.BlockSpec((1, hw), lambda b, h: (0, 0)),
                  pl.BlockSpec((ctx_len, hw), lambda b, h: (b, zb + h))],
        out_specs=pl.BlockSpec((ctx_len, hw), lambda b, h: (b, h)),
        compiler_params=_cparams("parallel", "arbitrary"),
        name="attn_b_context",
    )(pxc, pxc, pxc, lamqk, g_sub.reshape(1, hw), pxc)


def _attn_b_latent_kernel(q_ref, k_ref, v_ref, kc_ref, vc_ref, lamqk_ref, g_ref, z_ref, o_ref,
                          sa_scr, sb_scr, xa_scr, xb_scr, pa_scr, pb_scr, ala_scr, alb_scr,
                          m_scr, l_scr, acc_scr, *, lam_init, tk):
    nk = k_ref.shape[0] // tk
    tq = q_ref.shape[0]
    lanes = HEAD_DIM

    def lane_groups(x):
        return [x[:, g * lanes:(g + 1) * lanes] for g in range(x.shape[1] // lanes)]

    def group_max(s):
        return functools.reduce(jnp.maximum, lane_groups(s))

    def row_max(gmax):
        return jnp.broadcast_to(jnp.max(gmax, axis=-1, keepdims=True), (tq, lanes))

    nctx = kc_ref.shape[0]

    def scores(c, buf, with_ctx=False):
        s_scr, x_scr = buf[0], buf[1]
        start = pl.multiple_of(c * tk, tk)
        for mp in range(2):
            cols = slice(mp * HEAD_DIM, (mp + 1) * HEAD_DIM)
            s = _qkt(q_ref[:, cols], k_ref[pl.ds(start, tk), cols])
            s_scr[mp, :, 0:tk] = s
            gmax = group_max(s)
            if with_ctx:
                sc = _qkt(q_ref[:, cols], kc_ref[:, cols])
                s_scr[mp, :, tk:tk + nctx] = sc
                gmax = jnp.maximum(gmax, group_max(sc))
            x_scr[mp] = gmax

    def softmax(buf, first=False):
        s_scr, x_scr, p_scr, al_scr = buf
        width = tk + nctx if first else tk
        for mp in range(2):
            m_new = row_max(x_scr[mp])
            if not first:
                m_old = m_scr[mp]
                m_new = jnp.maximum(m_old, m_new)
                alpha = jnp.exp2(m_old - m_new)
            psum = None
            for g in range(width // lanes):
                cols = slice(g * lanes, (g + 1) * lanes)
                pg = jnp.exp2(s_scr[mp, :, cols] - m_new)
                p_scr[mp, :, cols] = pg.astype(p_scr.dtype)
                psum = pg if psum is None else psum + pg
            if first:
                l_scr[mp] = psum
            else:
                l_scr[mp] = alpha * l_scr[mp] + psum
                al_scr[mp] = alpha
            m_scr[mp] = m_new

    def weigh(c, buf, first=False):
        p_scr, al_scr = buf[2], buf[3]
        v = v_ref[pl.ds(pl.multiple_of(c * tk, tk), tk), :]
        for mp in range(2):
            pv = jnp.dot(p_scr[mp, :, 0:tk], v, preferred_element_type=F32)
            if first:
                acc_scr[mp] = pv + jnp.dot(p_scr[mp, :, tk:tk + nctx], vc_ref[...],
                                           preferred_element_type=F32)
            else:
                al = jnp.concatenate([al_scr[mp]] * (acc_scr.shape[2] // lanes), axis=1)
                acc_scr[mp] = al * acc_scr[mp] + pv

    buf_a = (sa_scr, xa_scr, pa_scr, ala_scr)
    buf_b = (sb_scr, xb_scr, pb_scr, alb_scr)
    bufs = (buf_a, buf_b)

    def steps(c0, n, parity):
        for t in range(n):
            cur, nxt = bufs[(parity + t) % 2], bufs[(parity + t + 1) % 2]
            weigh(c0 + t, cur)
            softmax(nxt)
            scores(c0 + t + 2, cur)

    scores(0, buf_a, with_ctx=True)
    softmax(buf_a, first=True)
    scores(1, buf_b)
    weigh(0, buf_a, first=True)
    softmax(buf_b)
    scores(2, buf_a)

    unroll = 4
    trips = (nk - 3) // unroll

    def body(i, carry):
        steps(1 + i * unroll, unroll, 1)
        return carry

    lax.fori_loop(0, trips, body, 0)
    steps(1 + trips * unroll, nk - 3 - trips * unroll, 1)
    weigh(nk - 2, buf_a)
    softmax(buf_b)
    weigh(nk - 1, buf_b)

    o = _diff_finish(acc_scr[0], _row_sum_lanes(l_scr[0]), acc_scr[1], _row_sum_lanes(l_scr[1]),
                     lamqk_ref, g_ref, lam_init)
    o_ref[...] = _gate_out(o, z_ref[...].astype(F32), o_ref.dtype)


def _attn_b_latent(pxl, pxc, lamqk, g_sub, *, lam_init, batch, seq, ctx_len, d, tq, tk):
    hw = 2 * HEAD_DIM
    heads = d // hw
    nq = seq // tq
    assert (seq // tk) % 2 == 0 and seq // tk >= 4 and ctx_len % HEAD_DIM == 0
    cols = _col_offsets(d)
    qb, zb = cols["q_b"] // hw, cols["z_b"] // hw
    kb_l = kb_c = cols["k_b"] // hw
    vb_l = vb_c = cols["v_b"] // hw
    return pl.pallas_call(
        functools.partial(_attn_b_latent_kernel, lam_init=lam_init, tk=tk),
        out_shape=jax.ShapeDtypeStruct((batch * seq, d), BF16),
        grid=(batch, heads, nq),
        in_specs=[pl.BlockSpec((tq, hw), lambda b, h, i: (b * nq + i, qb + h)),
                  pl.BlockSpec((seq, hw), lambda b, h, i: (b, kb_l + h)),
                  pl.BlockSpec((seq, hw), lambda b, h, i: (b, vb_l + h)),
                  pl.BlockSpec((ctx_len, hw), lambda b, h, i: (b, kb_c + h)),
                  pl.BlockSpec((ctx_len, hw), lambda b, h, i: (b, vb_c + h)),
                  pl.BlockSpec((4, HEAD_DIM), lambda b, h, i: (0, 0)),
                  pl.BlockSpec((1, hw), lambda b, h, i: (0, 0)),
                  pl.BlockSpec((tq, hw), lambda b, h, i: (b * nq + i, zb + h))],
        out_specs=pl.BlockSpec((tq, hw), lambda b, h, i: (b * nq + i, h)),
        scratch_shapes=[pltpu.VMEM((2, tq, tk + ctx_len), F32), pltpu.VMEM((2, tq, tk), F32),
                        pltpu.VMEM((2, tq, HEAD_DIM), F32), pltpu.VMEM((2, tq, HEAD_DIM), F32),
                        pltpu.VMEM((2, tq, tk + ctx_len), BF16), pltpu.VMEM((2, tq, tk), BF16),
                        pltpu.VMEM((2, tq, HEAD_DIM), F32), pltpu.VMEM((2, tq, HEAD_DIM), F32),
                        pltpu.VMEM((2, tq, HEAD_DIM), F32), pltpu.VMEM((2, tq, HEAD_DIM), F32),
                        pltpu.VMEM((2, tq, hw), F32)],
        compiler_params=_cparams("parallel", "parallel", "arbitrary"),
        name="attn_b_latent",
    )(pxl, pxl, pxl, pxc, pxc, lamqk, g_sub.reshape(1, hw), pxl)


def _merge1_kernel(a_ref, b_ref, ga0_ref, ga1_ref, gb0_ref, gb1_ref, wa_ref, wb_ref, u_ref):
    half = ga0_ref.shape[1]
    ya = jnp.dot(a_ref[...], wa_ref[...], preferred_element_type=F32)
    yb = jnp.dot(b_ref[...], wb_ref[...], preferred_element_type=F32)
    for part, (ga_ref, gb_ref) in enumerate(((ga0_ref, gb0_ref), (ga1_ref, gb1_ref))):
        cols = slice(part * half, (part + 1) * half)
        u = (_sigmoid(ga_ref[...].astype(F32)) * ya[:, cols]
             + _sigmoid(gb_ref[...].astype(F32)) * yb[:, cols])
        u_ref[:, cols] = u.astype(u_ref.dtype)


def _merge1(a, b, px, wpa_all, wpb_all, layer, *, d, tm):
    t = a.shape[0]
    half = d // 2
    cols = _col_offsets(d)
    assert cols["g_a"] % half == 0 and cols["g_b"] % half == 0
    ga, gb = cols["g_a"] // half, cols["g_b"] // half
    resident = pl.Buffered(1)
    return pl.pallas_call(
        _merge1_kernel,
        out_shape=jax.ShapeDtypeStruct((t, d), BF16),
        grid=(t // tm,),
        in_specs=[pl.BlockSpec((tm, d), lambda i: (i, 0)),
                  pl.BlockSpec((tm, d), lambda i: (i, 0)),
                  pl.BlockSpec((tm, half), lambda i: (i, ga)),
                  pl.BlockSpec((tm, half), lambda i: (i, ga + 1)),
                  pl.BlockSpec((tm, half), lambda i: (i, gb)),
                  pl.BlockSpec((tm, half), lambda i: (i, gb + 1)),
                  pl.BlockSpec((None, d, d), lambda i: (layer, 0, 0), pipeline_mode=resident),
                  pl.BlockSpec((None, d, d), lambda i: (layer, 0, 0), pipeline_mode=resident)],
        out_specs=pl.BlockSpec((tm, d), lambda i: (i, 0)),
        compiler_params=_cparams("parallel"),
        name="merge_gate",
    )(a, b, px, px, px, px, wpa_all, wpb_all)


def _merge2_kernel(u_ref, wo_ref, x_ref, g_ref, mod_ref, o_ref, *, d):
    y = jnp.dot(u_ref[...], wo_ref[...], preferred_element_type=F32)
    n = y * lax.rsqrt(jnp.mean(y * y, axis=-1, keepdims=True) + EPS) * g_ref[...]
    o_ref[...] = x_ref[...] + mod_ref[:, 2 * d:3 * d] * n


def _merge2(u, wo_all, layer, x2, g_post, mod3, row_of_tile, tm):
    t, d = x2.shape
    return pl.pallas_call(
        functools.partial(_merge2_kernel, d=d),
        out_shape=jax.ShapeDtypeStruct((t, d), F32),
        grid=(t // tm,),
        in_specs=[pl.BlockSpec((tm, d), lambda i: (i, 0)),
                  pl.BlockSpec((None, d, d), lambda i: (layer, 0, 0)),
                  pl.BlockSpec((tm, d), lambda i: (i, 0)),
                  pl.BlockSpec((1, d), lambda i: (0, 0)),
                  pl.BlockSpec((None, 1, 3 * d), lambda i: (row_of_tile(i), 0, 0))],
        out_specs=pl.BlockSpec((tm, d), lambda i: (i, 0)),
        compiler_params=_cparams("parallel"),
        name="merge_out",
    )(u, wo_all, x2, g_post.reshape(1, d), mod3)


def _rope_tables(seq):
    rows = seq // GRID_W
    r = jnp.repeat(jnp.arange(rows, dtype=F32), GRID_W)
    col = jnp.tile(jnp.arange(GRID_W, dtype=F32), rows)
    n_freq = HEAD_DIM // 4
    inv = ROPE_THETA ** (-jnp.arange(n_freq, dtype=F32) / n_freq)
    ang = jnp.concatenate([r[:, None] * inv, col[:, None] * inv], axis=-1)
    cos, sin = jnp.cos(ang), jnp.sin(ang)
    return jnp.concatenate([cos, cos], axis=-1), jnp.concatenate([-sin, sin], axis=-1)


def _tile_types(d, tn, rope):
    cols = _col_offsets(d)
    r = 1 if rope else 0
    kinds = {"k_a": r, "k_b": r, "q_a": r | 2 | 4, "q_b": r | 2 | 4}
    out = [0] * (cols["in_cols"] // tn)
    for name, ty in kinds.items():
        width = d // A_GROUP if name == "k_a" else d
        assert cols[name] % tn == 0 and width % tn == 0
        for t in range(cols[name] // tn, (cols[name] + width) // tn):
            out[t] = ty
    return jnp.asarray(out, jnp.int32)


def kernel(x, c, ctx, c_ctx, w_ada, b_ada, g_pre, g_post, w_in, sink, lam_qk, g_subln,
           w_proj_a, w_proj_b, w_out):
    batch, seq, d = x.shape
    ctx_len = ctx.shape[1]
    depth = w_in.shape[0]
    t_lat, t_ctx = batch * seq, batch * ctx_len
    cols = _col_offsets(d)
    kv_cols, in_cols = cols["kv_cols"], cols["in_cols"]

    tn_in = min(512, d // A_GROUP)
    tm_lat = _pick(seq, (1024, 512, 256))
    tm_in = _pick(seq, (2048, 1024, 512, 256))
    tm_ctx = _pick(t_ctx, (512, 256))
    tq_a = _pick(seq, (1024, 512, 256, 128))
    tq_b = _pick(seq, (1024, 512, 256))
    tk_b = _pick(seq // 4, (512, 256))
    tm_m = _pick(seq, (512, 256))

    cosf, sinf = _rope_tables(seq)
    zeros_tab = jnp.zeros((tm_ctx, HEAD_DIM), F32)
    types_lat = _tile_types(d, tn_in, rope=True)
    types_ctx = _tile_types(d, tn_in, rope=False)
    cvec8 = jnp.concatenate([c, c_ctx[None, :], jnp.zeros((8 - batch - 1, d), F32)], axis=0)

    wpa, wpb, wo = w_proj_a.astype(BF16), w_proj_b.astype(BF16), w_out.astype(BF16)
    x2 = x.reshape(t_lat, d)
    c2 = ctx.reshape(t_ctx, d)
    lat_row = lambda tm: (lambda i: i // (seq // tm))
    ctx_row = lambda i: batch

    for l in range(depth):
        last = l == depth - 1
        lam_init = 0.8 - 0.6 * math.exp(-0.3 * l)
        mod3 = _ada(cvec8, w_ada, b_ada, l).reshape(8, 1, 3 * d)

        hx = _prenorm(x2, g_pre[l], mod3, lat_row(tm_lat), tm_lat)
        hc = _prenorm(c2, g_pre[l], mod3, ctx_row, tm_ctx)
        pxl = _inproj(hx, w_in, l, types_lat, cosf, sinf, ncols=in_cols,
                      tm=tm_in, tn=tn_in, pos_tiles=seq // tm_in)
        pxc = _inproj(hc, w_in, l, types_ctx, zeros_tab, zeros_tab, ncols=kv_cols if last else in_cols,
                      tm=tm_ctx, tn=tn_in, pos_tiles=1)

        o_a = _attn_a(pxl, pxc, sink[l], batch=batch, seq=seq, ctx_len=ctx_len, d=d, tq=tq_a)
        o_b = _attn_b_latent(pxl, pxc, lam_qk[l], g_subln[l], lam_init=lam_init, batch=batch,
                             seq=seq, ctx_len=ctx_len, d=d, tq=tq_b, tk=tk_b)
        u = _merge1(o_a, o_b, pxl, wpa, wpb, l, d=d, tm=tm_m)

        if not last:
            oc_a = _attn_a_ctx(pxc, sink[l], batch=batch, ctx_len=ctx_len, d=d)
            oc_b = _attn_b_ctx(pxc, lam_qk[l], g_subln[l], lam_init=lam_init, batch=batch,
                               ctx_len=ctx_len, d=d)
            uc = _merge1(oc_a, oc_b, pxc, wpa, wpb, l, d=d, tm=tm_ctx)
            c2 = _merge2(uc, wo, l, c2, g_post[l], mod3, ctx_row, tm_ctx)

        x2 = _merge2(u, wo, l, x2, g_post[l], mod3, lat_row(tm_m), tm_m)

    return x2.reshape(batch, seq, d)
```

```python
import functools
import math

import jax
import jax.numpy as jnp
from jax import lax
from jax.experimental import pallas as pl
from jax.experimental.pallas import tpu as pltpu

HEAD_DIM = 128
GRID_W = 64
WINDOW = 128
A_GROUP = 4
ROPE_THETA = 10000.0
EPS = 1e-6
NEG = -1e30
Q_SCALE = HEAD_DIM ** -0.5
LOG2E = math.log2(math.e)
VMEM_LIMIT_BYTES = 56 * 1024 * 1024

F32 = jnp.float32
BF16 = jnp.bfloat16


def _cparams(*semantics):
    return pltpu.CompilerParams(dimension_semantics=semantics, vmem_limit_bytes=VMEM_LIMIT_BYTES)


def _pick(n, candidates):
    for c in candidates:
        if n % c == 0:
            return c
    raise ValueError(f"no tile in {candidates} divides {n}")


def _sigmoid(v):
    return 0.5 * jnp.tanh(0.5 * v) + 0.5


def _silu(v):
    h = 0.5 * v
    return h * jnp.tanh(h) + h


def _gate_out(o, z, dtype):
    return (o * _silu(z)).astype(dtype)


def _col_offsets(d):
    kv = d // A_GROUP
    names = ("k_a", "v_a", "k_b", "v_b", "q_a", "z_a", "q_b", "z_b", "g_a", "g_b")
    widths = (kv, kv, d, d, d, d, d, d, d, d)
    off, out = 0, {}
    for n, w in zip(names, widths):
        out[n] = off
        off += w
    out["kv_cols"] = out["q_a"]
    out["in_cols"] = off
    return out


def _ada_kernel(c_ref, w_ref, b_ref, o_ref):
    a = _silu(c_ref[...])
    o_ref[...] = jnp.dot(a, w_ref[...], preferred_element_type=F32,
                         precision=lax.Precision.HIGHEST) + b_ref[...]


def _ada(cvec8, w_all, b_all, layer):
    depth, d, n = w_all.shape
    tn = _pick(n, (1024, 768, 512, 384, 256, 128))
    return pl.pallas_call(
        _ada_kernel,
        out_shape=jax.ShapeDtypeStruct((8, n), F32),
        grid=(n // tn,),
        in_specs=[pl.BlockSpec((8, d), lambda j: (0, 0)),
                  pl.BlockSpec((None, d, tn), lambda j: (layer, 0, j)),
                  pl.BlockSpec((None, 1, tn), lambda j: (layer, 0, j))],
        out_specs=pl.BlockSpec((8, tn), lambda j: (0, j)),
        compiler_params=_cparams("arbitrary"),
        name="ada_ln",
    )(cvec8, w_all, b_all.reshape(depth, 1, n))


def _prenorm_kernel(x_ref, g_ref, mod_ref, o_ref, *, d):
    x = x_ref[...]
    y = x * lax.rsqrt(jnp.mean(x * x, axis=-1, keepdims=True) + EPS)
    y = y * g_ref[...]
    shift = mod_ref[:, 0:d]
    scale = mod_ref[:, d:2 * d]
    o_ref[...] = (y * (1.0 + scale) + shift).astype(o_ref.dtype)


def _prenorm(x2, g, mod3, row_of_tile, tm):
    t, d = x2.shape
    return pl.pallas_call(
        functools.partial(_prenorm_kernel, d=d),
        out_shape=jax.ShapeDtypeStruct((t, d), BF16),
        grid=(t // tm,),
        in_specs=[pl.BlockSpec((tm, d), lambda i: (i, 0)),
                  pl.BlockSpec((1, d), lambda i: (0, 0)),
                  pl.BlockSpec((None, 1, 3 * d), lambda i: (row_of_tile(i), 0, 0))],
        out_specs=pl.BlockSpec((tm, d), lambda i: (i, 0)),
        compiler_params=_cparams("parallel"),
        name="prenorm",
    )(x2, g.reshape(1, d), mod3)


def _inproj_kernel(type_ref, h_ref, w_ref, cos_ref, sin_ref, o_ref):
    t = type_ref[pl.program_id(1)]
    heads = o_ref.shape[1] // HEAD_DIM

    def matmul():
        return jnp.dot(h_ref[...], w_ref[...].astype(h_ref.dtype), preferred_element_type=F32)

    @pl.when(t == 0)
    def _():
        o_ref[...] = matmul().astype(o_ref.dtype)

    @pl.when(t != 0)
    def _():
        acc = matmul()
        rope = (t & 1).astype(F32)
        scale = (jnp.where((t & 2) != 0, Q_SCALE, 1.0) * jnp.where((t & 4) != 0, LOG2E, 1.0)).astype(F32)
        ca = (rope * cos_ref[...] + (1.0 - rope)) * scale
        sa = (rope * sin_ref[...]) * scale
        for hh in range(heads):
            a = acc[:, hh * HEAD_DIM:(hh + 1) * HEAD_DIM]
            r = a * ca + pltpu.roll(a, HEAD_DIM // 2, 1) * sa
            o_ref[:, hh * HEAD_DIM:(hh + 1) * HEAD_DIM] = r.astype(o_ref.dtype)


def _inproj(h, w_all, layer, types, cosf, sinf, *, ncols, tm, tn, pos_tiles):
    t, d = h.shape
    grid_spec = pltpu.PrefetchScalarGridSpec(
        num_scalar_prefetch=1,
        grid=(t // tm, ncols // tn),
        in_specs=[pl.BlockSpec((tm, d), lambda i, j, ty: (i, 0)),
                  pl.BlockSpec((None, d, tn), lambda i, j, ty: (layer, 0, j)),
                  pl.BlockSpec((tm, HEAD_DIM), lambda i, j, ty: (i % pos_tiles, 0)),
                  pl.BlockSpec((tm, HEAD_DIM), lambda i, j, ty: (i % pos_tiles, 0))],
        out_specs=pl.BlockSpec((tm, tn), lambda i, j, ty: (i, j)),
    )
    return pl.pallas_call(
        _inproj_kernel,
        out_shape=jax.ShapeDtypeStruct((t, ncols), BF16),
        grid_spec=grid_spec,
        compiler_params=_cparams("parallel", "arbitrary"),
        name="in_proj",
    )(types, h, w_all, cosf, sinf)


def _stack_heads(q):
    g = q.shape[1] // HEAD_DIM
    return jnp.concatenate([q[:, i * HEAD_DIM:(i + 1) * HEAD_DIM] for i in range(g)], axis=0)


def _unstack_heads(o, g):
    rows = o.shape[0] // g
    return jnp.concatenate([o[i * rows:(i + 1) * rows, :] for i in range(g)], axis=1)


def _sink_column(sink_ref, kvh, rows):
    return jnp.concatenate(
        [jnp.full((rows, 1), sink_ref[kvh * A_GROUP + g], F32) for g in range(A_GROUP)], axis=0)


def _qkt(q, k):
    return lax.dot_general(q, k, (((1,), (1,)), ((), ())), preferred_element_type=F32)


def _attn_a_kernel(sink_ref, q_ref, kp_ref, km_ref, kn_ref, vp_ref, vm_ref, vn_ref, kc_ref, vc_ref,
                   z_ref, o_ref, *, seq):
    i = pl.program_id(1)
    kvh = pl.program_id(2)
    tq = q_ref.shape[0]
    k_win = jnp.concatenate([kp_ref[...], km_ref[...], kn_ref[...]], axis=0)
    v_win = jnp.concatenate([vp_ref[...], vm_ref[...], vn_ref[...]], axis=0)
    kc = kc_ref[...]
    vc = vc_ref[...]
    rows = A_GROUP * WINDOW
    nloc = 3 * WINDOW
    nkeys = nloc + kc.shape[0]
    sink = _sink_column(sink_ref, kvh, WINDOW) * LOG2E
    qi = lax.broadcasted_iota(jnp.int32, (rows, nkeys), 0) & (WINDOW - 1)
    kj = lax.broadcasted_iota(jnp.int32, (rows, nkeys), 1)
    band = jnp.abs(qi + WINDOW - kj) <= WINDOW
    is_ctx = kj >= nloc
    nsub = tq // WINDOW

    def scores(r):
        q = _stack_heads(q_ref[r * WINDOW:(r + 1) * WINDOW, :])
        k = jnp.concatenate([k_win[r * WINDOW:(r + 3) * WINDOW, :], kc], axis=0)
        base = i * tq + (r - 1) * WINDOW
        valid = is_ctx | (band & (kj >= -base) & (kj < seq - base))
        return jnp.where(valid, _qkt(q, k), NEG)

    def lane_bcast(col):
        return jnp.broadcast_to(col, (rows, HEAD_DIM))

    sink_b = lane_bcast(sink)

    def softmax(s):
        groups = [s[:, g * HEAD_DIM:(g + 1) * HEAD_DIM] for g in range(nkeys // HEAD_DIM)]
        gmax = functools.reduce(jnp.maximum, groups)
        m = jnp.maximum(lane_bcast(jnp.max(gmax, axis=-1, keepdims=True)), sink_b)
        ps = [jnp.exp2(sg - m) for sg in groups]
        psum = functools.reduce(jnp.add, ps)
        denom = lane_bcast(jnp.sum(psum, axis=-1, keepdims=True)) + jnp.exp2(sink_b - m)
        return jnp.concatenate([pg.astype(vc.dtype) for pg in ps], axis=1), denom

    def weigh(r, p, denom):
        v = jnp.concatenate([v_win[r * WINDOW:(r + 3) * WINDOW, :], vc], axis=0)
        o = _unstack_heads(jnp.dot(p, v, preferred_element_type=F32) / denom, A_GROUP)
        z = z_ref[r * WINDOW:(r + 1) * WINDOW, :].astype(F32)
        o_ref[r * WINDOW:(r + 1) * WINDOW, :] = _gate_out(o, z, o_ref.dtype)

    s_next = scores(0)
    pd_next = None
    for r in range(nsub + 2):
        pd_cur, pd_next = pd_next, None
        s_cur, s_next = s_next, None
        if r < nsub - 1:
            s_next = scores(r + 1)
        if s_cur is not None and r < nsub:
            pd_next = softmax(s_cur)
        if pd_cur is not None:
            weigh(r - 1, *pd_cur)


def _attn_a(pxl, pxc, sink, *, batch, seq, ctx_len, d, tq):
    kvh = d // HEAD_DIM // A_GROUP
    gw = A_GROUP * HEAD_DIM
    nq = seq // tq
    r = tq // WINDOW
    nblk = seq // WINDOW
    cols = _col_offsets(d)
    assert cols["q_a"] % gw == 0
    assert cols["z_a"] % gw == 0
    qa, za = cols["q_a"] // gw, cols["z_a"] // gw
    ka_l = ka_c = cols["k_a"] // HEAD_DIM
    va_l = va_c = cols["v_a"] // HEAD_DIM

    def prev(b, i, h):
        return (b * nblk + jnp.maximum(i * r - 1, 0), h)

    def nxt(b, i, h):
        return (b * nblk + jnp.minimum((i + 1) * r, nblk - 1), h)

    halo = (WINDOW, HEAD_DIM)
    return pl.pallas_call(
        functools.partial(_attn_a_kernel, seq=seq),
        out_shape=jax.ShapeDtypeStruct((batch * seq, d), BF16),
        grid=(batch, nq, kvh),
        in_specs=[
            pl.BlockSpec(memory_space=pltpu.SMEM),
            pl.BlockSpec((tq, gw), lambda b, i, h: (b * nq + i, qa + h)),
            pl.BlockSpec(halo, lambda b, i, h: prev(b, i, ka_l + h)),
            pl.BlockSpec((tq, HEAD_DIM), lambda b, i, h: (b * nq + i, ka_l + h)),
            pl.BlockSpec(halo, lambda b, i, h: nxt(b, i, ka_l + h)),
            pl.BlockSpec(halo, lambda b, i, h: prev(b, i, va_l + h)),
            pl.BlockSpec((tq, HEAD_DIM), lambda b, i, h: (b * nq + i, va_l + h)),
            pl.BlockSpec(halo, lambda b, i, h: nxt(b, i, va_l + h)),
            pl.BlockSpec((ctx_len, HEAD_DIM), lambda b, i, h: (b, ka_c + h)),
            pl.BlockSpec((ctx_len, HEAD_DIM), lambda b, i, h: (b, va_c + h)),
            pl.BlockSpec((tq, gw), lambda b, i, h: (b * nq + i, za + h)),
        ],
        out_specs=pl.BlockSpec((tq, gw), lambda b, i, h: (b * nq + i, h)),
        compiler_params=_cparams("parallel", "parallel", "arbitrary"),
        name="attn_a_latent",
    )(sink, pxl, pxl, pxl, pxl, pxl, pxl, pxl, pxc, pxc, pxl)


def _attn_a_ctx_kernel(sink_ref, q_ref, k_ref, v_ref, z_ref, o_ref):
    kvh = pl.program_id(1)
    rows = q_ref.shape[0]
    q = _stack_heads(q_ref[...])
    v = v_ref[...]
    sink = _sink_column(sink_ref, kvh, rows) * LOG2E
    s = _qkt(q, k_ref[...])
    m = jnp.maximum(jnp.max(s, axis=-1, keepdims=True), sink)
    p = jnp.exp2(s - m)
    denom = jnp.sum(p, axis=-1, keepdims=True) + jnp.exp2(sink - m)
    o = _unstack_heads(jnp.dot(p.astype(v.dtype), v, preferred_element_type=F32) / denom, A_GROUP)
    o_ref[...] = _gate_out(o, z_ref[...].astype(F32), o_ref.dtype)


def _attn_a_ctx(pxc, sink, *, batch, ctx_len, d):
    kvh = d // HEAD_DIM // A_GROUP
    gw = A_GROUP * HEAD_DIM
    cols = _col_offsets(d)
    qa, za = cols["q_a"] // gw, cols["z_a"] // gw
    ka_c, va_c = cols["k_a"] // HEAD_DIM, cols["v_a"] // HEAD_DIM
    return pl.pallas_call(
        _attn_a_ctx_kernel,
        out_shape=jax.ShapeDtypeStruct((batch * ctx_len, d), BF16),
        grid=(batch, kvh),
        in_specs=[pl.BlockSpec(memory_space=pltpu.SMEM),
                  pl.BlockSpec((ctx_len, gw), lambda b, h: (b, qa + h)),
                  pl.BlockSpec((ctx_len, HEAD_DIM), lambda b, h: (b, ka_c + h)),
                  pl.BlockSpec((ctx_len, HEAD_DIM), lambda b, h: (b, va_c + h)),
                  pl.BlockSpec((ctx_len, gw), lambda b, h: (b, za + h))],
        out_specs=pl.BlockSpec((ctx_len, gw), lambda b, h: (b, h)),
        compiler_params=_cparams("parallel", "arbitrary"),
        name="attn_a_context",
    )(sink, pxc, pxc, pxc, pxc)


def _row_sum_lanes(x):
    groups = [x[:, g * HEAD_DIM:(g + 1) * HEAD_DIM] for g in range(x.shape[1] // HEAD_DIM)]
    part = functools.reduce(jnp.add, groups)
    return jnp.broadcast_to(jnp.sum(part, axis=-1, keepdims=True), part.shape)


def _diff_finish(o0, l0, o1, l1, lamqk_ref, g_ref, lam_init):
    lq = lamqk_ref[...]
    lam = (jnp.exp(jnp.sum(lq[0:1] * lq[1:2], axis=-1, keepdims=True))
           - jnp.exp(jnp.sum(lq[2:3] * lq[3:4], axis=-1, keepdims=True)) + lam_init)
    wide = lambda t: jnp.concatenate([t] * (o0.shape[1] // HEAD_DIM), axis=1)
    o = o0 / wide(l0) - lam * (o1 / wide(l1))
    inv = lax.rsqrt(_row_sum_lanes(o * o) * (1.0 / o.shape[1]) + EPS)
    return o * wide(inv) * g_ref[...] * (1.0 - lam_init)


def _attn_b_ctx_kernel(q_ref, k_ref, v_ref, lamqk_ref, g_ref, z_ref, o_ref, *, lam_init):
    v = v_ref[...]
    parts = []
    for mp in range(2):
        cols = slice(mp * HEAD_DIM, (mp + 1) * HEAD_DIM)
        s = _qkt(q_ref[:, cols], k_ref[:, cols])
        p = jnp.exp2(s - jnp.max(s, axis=-1, keepdims=True))
        parts += [jnp.dot(p.astype(v.dtype), v, preferred_element_type=F32), _row_sum_lanes(p)]
    o = _diff_finish(*parts, lamqk_ref, g_ref, lam_init)
    o_ref[...] = _gate_out(o, z_ref[...].astype(F32), o_ref.dtype)


def _attn_b_ctx(pxc, lamqk, g_sub, *, lam_init, batch, ctx_len, d):
    hw = 2 * HEAD_DIM
    heads = d // hw
    cols = _col_offsets(d)
    qb, kb, vb, zb = cols["q_b"] // hw, cols["k_b"] // hw, cols["v_b"] // hw, cols["z_b"] // hw
    return pl.pallas_call(
        functools.partial(_attn_b_ctx_kernel, lam_init=lam_init),
        out_shape=jax.ShapeDtypeStruct((batch * ctx_len, d), BF16),
        grid=(batch, heads),
        in_specs=[pl.BlockSpec((ctx_len, hw), lambda b, h: (b, qb + h)),
                  pl.BlockSpec((ctx_len, hw), lambda b, h: (b, kb + h)),
                  pl.BlockSpec((ctx_len, hw), lambda b, h: (b, vb + h)),
                  pl.BlockSpec((4, HEAD_DIM), lambda b, h: (0, 0)),
                  pl.BlockSpec((1, hw), lambda b, h: (0, 0)),
                  pl.BlockSpec((ctx_len, hw), lambda b, h: (b, zb + h))],
        out_specs=pl.BlockSpec((ctx_len, hw), lambda b, h: (b, h)),
        compiler_params=_cparams("parallel", "arbitrary"),
        name="attn_b_context",
    )(pxc, pxc, pxc, lamqk, g_sub.reshape(1, hw), pxc)


def _attn_b_latent_kernel(q_ref, k_ref, v_ref, kc_ref, vc_ref, lamqk_ref, g_ref, z_ref, o_ref,
                          sa_scr, sb_scr, xa_scr, xb_scr, pa_scr, pb_scr, ala_scr, alb_scr,
                          m_scr, l_scr, acc_scr, *, lam_init, tk):
    nk = k_ref.shape[0] // tk
    tq = q_ref.shape[0]
    lanes = HEAD_DIM

    def lane_groups(x):
        return [x[:, g * lanes:(g + 1) * lanes] for g in range(x.shape[1] // lanes)]

    def group_max(s):
        return functools.reduce(jnp.maximum, lane_groups(s))

    def row_max(gmax):
        return jnp.broadcast_to(jnp.max(gmax, axis=-1, keepdims=True), (tq, lanes))

    nctx = kc_ref.shape[0]

    def scores(c, buf, with_ctx=False):
        s_scr, x_scr = buf[0], buf[1]
        start = pl.multiple_of(c * tk, tk)
        for mp in range(2):
            cols = slice(mp * HEAD_DIM, (mp + 1) * HEAD_DIM)
            s = _qkt(q_ref[:, cols], k_ref[pl.ds(start, tk), cols])
            s_scr[mp, :, 0:tk] = s
            gmax = group_max(s)
            if with_ctx:
                sc = _qkt(q_ref[:, cols], kc_ref[:, cols])
                s_scr[mp, :, tk:tk + nctx] = sc
                gmax = jnp.maximum(gmax, group_max(sc))
            x_scr[mp] = gmax

    def softmax(buf, first=False):
        s_scr, x_scr, p_scr, al_scr = buf
        width = tk + nctx if first else tk
        for mp in range(2):
            m_new = row_max(x_scr[mp])
            if not first:
                m_old = m_scr[mp]
                m_new = jnp.maximum(m_old, m_new)
                alpha = jnp.exp2(m_old - m_new)
            psum = None
            for g in range(width // lanes):
                cols = slice(g * lanes, (g + 1) * lanes)
                pg = jnp.exp2(s_scr[mp, :, cols] - m_new)
                p_scr[mp, :, cols] = pg.astype(p_scr.dtype)
                psum = pg if psum is None else psum + pg
            if first:
                l_scr[mp] = psum
            else:
                l_scr[mp] = alpha * l_scr[mp] + psum
                al_scr[mp] = alpha
            m_scr[mp] = m_new

    def weigh(c, buf, first=False):
        p_scr, al_scr = buf[2], buf[3]
        v = v_ref[pl.ds(pl.multiple_of(c * tk, tk), tk), :]
        for mp in range(2):
            pv = jnp.dot(p_scr[mp, :, 0:tk], v, preferred_element_type=F32)
            if first:
                acc_scr[mp] = pv + jnp.dot(p_scr[mp, :, tk:tk + nctx], vc_ref[...],
                                           preferred_element_type=F32)
            else:
                al = jnp.concatenate([al_scr[mp]] * (acc_scr.shape[2] // lanes), axis=1)
                acc_scr[mp] = al * acc_scr[mp] + pv

    buf_a = (sa_scr, xa_scr, pa_scr, ala_scr)
    buf_b = (sb_scr, xb_scr, pb_scr, alb_scr)
    bufs = (buf_a, buf_b)

    def steps(c0, n, parity):
        for t in range(n):
            cur, nxt = bufs[(parity + t) % 2], bufs[(parity + t + 1) % 2]
            weigh(c0 + t, cur)
            softmax(nxt)
            scores(c0 + t + 2, cur)

    scores(0, buf_a, with_ctx=True)
    softmax(buf_a, first=True)
    scores(1, buf_b)
    weigh(0, buf_a, first=True)
    softmax(buf_b)
    scores(2, buf_a)

    unroll = 4
    trips = (nk - 3) // unroll

    def body(i, carry):
        steps(1 + i * unroll, unroll, 1)
        return carry

    lax.fori_loop(0, trips, body, 0)
    steps(1 + trips * unroll, nk - 3 - trips * unroll, 1)
    weigh(nk - 2, buf_a)
    softmax(buf_b)
    weigh(nk - 1, buf_b)

    o = _diff_finish(acc_scr[0], _row_sum_lanes(l_scr[0]), acc_scr[1], _row_sum_lanes(l_scr[1]),
                     lamqk_ref, g_ref, lam_init)
    o_ref[...] = _gate_out(o, z_ref[...].astype(F32), o_ref.dtype)


def _attn_b_latent(pxl, pxc, lamqk, g_sub, *, lam_init, batch, seq, ctx_len, d, tq, tk):
    hw = 2 * HEAD_DIM
    heads = d // hw
    nq = seq // tq
    assert (seq // tk) % 2 == 0 and seq // tk >= 4 and ctx_len % HEAD_DIM == 0
    cols = _col_offsets(d)
    qb, zb = cols["q_b"] // hw, cols["z_b"] // hw
    kb_l = kb_c = cols["k_b"] // hw
    vb_l = vb_c = cols["v_b"] // hw
    return pl.pallas_call(
        functools.partial(_attn_b_latent_kernel, lam_init=lam_init, tk=tk),
        out_shape=jax.ShapeDtypeStruct((batch * seq, d), BF16),
        grid=(batch, heads, nq),
        in_specs=[pl.BlockSpec((tq, hw), lambda b, h, i: (b * nq + i, qb + h)),
                  pl.BlockSpec((seq, hw), lambda b, h, i: (b, kb_l + h)),
                  pl.BlockSpec((seq, hw), lambda b, h, i: (b, vb_l + h)),
                  pl.BlockSpec((ctx_len, hw), lambda b, h, i: (b, kb_c + h)),
                  pl.BlockSpec((ctx_len, hw), lambda b, h, i: (b, vb_c + h)),
                  pl.BlockSpec((4, HEAD_DIM), lambda b, h, i: (0, 0)),
                  pl.BlockSpec((1, hw), lambda b, h, i: (0, 0)),
                  pl.BlockSpec((tq, hw), lambda b, h, i: (b * nq + i, zb + h))],
        out_specs=pl.BlockSpec((tq, hw), lambda b, h, i: (b * nq + i, h)),
        scratch_shapes=[pltpu.VMEM((2, tq, tk + ctx_len), F32), pltpu.VMEM((2, tq, tk), F32),
                        pltpu.VMEM((2, tq, HEAD_DIM), F32), pltpu.VMEM((2, tq, HEAD_DIM), F32),
                        pltpu.VMEM((2, tq, tk + ctx_len), BF16), pltpu.VMEM((2, tq, tk), BF16),
                        pltpu.VMEM((2, tq, HEAD_DIM), F32), pltpu.VMEM((2, tq, HEAD_DIM), F32),
                        pltpu.VMEM((2, tq, HEAD_DIM), F32), pltpu.VMEM((2, tq, HEAD_DIM), F32),
                        pltpu.VMEM((2, tq, hw), F32)],
        compiler_params=_cparams("parallel", "parallel", "arbitrary"),
        name="attn_b_latent",
    )(pxl, pxl, pxl, pxc, pxc, lamqk, g_sub.reshape(1, hw), pxl)


def _merge1_kernel(a_ref, b_ref, ga0_ref, ga1_ref, gb0_ref, gb1_ref, wa_ref, wb_ref, u_ref):
    half = ga0_ref.shape[1]
    ya = jnp.dot(a_ref[...], wa_ref[...], preferred_element_type=F32)
    yb = jnp.dot(b_ref[...], wb_ref[...], preferred_element_type=F32)
    for part, (ga_ref, gb_ref) in enumerate(((ga0_ref, gb0_ref), (ga1_ref, gb1_ref))):
        cols = slice(part * half, (part + 1) * half)
        u = (_sigmoid(ga_ref[...].astype(F32)) * ya[:, cols]
             + _sigmoid(gb_ref[...].astype(F32)) * yb[:, cols])
        u_ref[:, cols] = u.astype(u_ref.dtype)


def _merge1(a, b, px, wpa, wpb, *, d, tm):
    t = a.shape[0]
    half = d // 2
    cols = _col_offsets(d)
    assert cols["g_a"] % half == 0 and cols["g_b"] % half == 0
    ga, gb = cols["g_a"] // half, cols["g_b"] // half
    resident = pl.Buffered(1)
    return pl.pallas_call(
        _merge1_kernel,
        out_shape=jax.ShapeDtypeStruct((t, d), BF16),
        grid=(t // tm,),
        in_specs=[pl.BlockSpec((tm, d), lambda i: (i, 0)),
                  pl.BlockSpec((tm, d), lambda i: (i, 0)),
                  pl.BlockSpec((tm, half), lambda i: (i, ga)),
                  pl.BlockSpec((tm, half), lambda i: (i, ga + 1)),
                  pl.BlockSpec((tm, half), lambda i: (i, gb)),
                  pl.BlockSpec((tm, half), lambda i: (i, gb + 1)),
                  pl.BlockSpec((d, d), lambda i: (0, 0), pipeline_mode=resident),
                  pl.BlockSpec((d, d), lambda i: (0, 0), pipeline_mode=resident)],
        out_specs=pl.BlockSpec((tm, d), lambda i: (i, 0)),
        compiler_params=_cparams("parallel"),
        name="merge_gate",
    )(a, b, px, px, px, px, wpa, wpb)


def _merge2_kernel(u_ref, wo_ref, x_ref, g_ref, mod_ref, *rest, d, with_next):
    if with_next:
        gn_ref, modn_ref, o_ref, h_ref = rest
    else:
        (o_ref,) = rest
    y = jnp.dot(u_ref[...], wo_ref[...], preferred_element_type=F32)
    n = y * lax.rsqrt(jnp.mean(y * y, axis=-1, keepdims=True) + EPS) * g_ref[...]
    x_new = x_ref[...] + mod_ref[:, 2 * d:3 * d] * n
    o_ref[...] = x_new
    if with_next:
        hn = x_new * lax.rsqrt(jnp.mean(x_new * x_new, axis=-1, keepdims=True) + EPS) * gn_ref[...]
        h_ref[...] = (hn * (1.0 + modn_ref[:, d:2 * d]) + modn_ref[:, 0:d]).astype(h_ref.dtype)


def _merge2(u, wo, x2, g_post, mod3, row_of_tile, tm, g_pre_next=None, mod3_next=None):
    t, d = x2.shape
    with_next = g_pre_next is not None
    mod_spec = pl.BlockSpec((None, 1, 3 * d), lambda i: (row_of_tile(i), 0, 0))
    row_spec = pl.BlockSpec((tm, d), lambda i: (i, 0))
    vec_spec = pl.BlockSpec((1, d), lambda i: (0, 0))
    in_specs = [row_spec, pl.BlockSpec((d, d), lambda i: (0, 0)), row_spec, vec_spec, mod_spec]
    args = [u, wo, x2, g_post.reshape(1, d), mod3]
    out_shape, out_specs = jax.ShapeDtypeStruct((t, d), F32), row_spec
    if with_next:
        in_specs += [vec_spec, mod_spec]
        args += [g_pre_next.reshape(1, d), mod3_next]
        out_shape = (out_shape, jax.ShapeDtypeStruct((t, d), BF16))
        out_specs = (row_spec, row_spec)
    return pl.pallas_call(
        functools.partial(_merge2_kernel, d=d, with_next=with_next),
        out_shape=out_shape,
        grid=(t // tm,),
        in_specs=in_specs,
        out_specs=out_specs,
        compiler_params=_cparams("parallel"),
        name="merge_out",
    )(*args)


def _rope_tables(seq):
    rows = seq // GRID_W
    r = jnp.repeat(jnp.arange(rows, dtype=F32), GRID_W)
    col = jnp.tile(jnp.arange(GRID_W, dtype=F32), rows)
    n_freq = HEAD_DIM // 4
    inv = ROPE_THETA ** (-jnp.arange(n_freq, dtype=F32) / n_freq)
    ang = jnp.concatenate([r[:, None] * inv, col[:, None] * inv], axis=-1)
    cos, sin = jnp.cos(ang), jnp.sin(ang)
    return jnp.concatenate([cos, cos], axis=-1), jnp.concatenate([-sin, sin], axis=-1)


def _tile_types(d, tn, rope):
    cols = _col_offsets(d)
    r = 1 if rope else 0
    kinds = {"k_a": r, "k_b": r, "q_a": r | 2 | 4, "q_b": r | 2 | 4}
    out = [0] * (cols["in_cols"] // tn)
    for name, ty in kinds.items():
        width = d // A_GROUP if name == "k_a" else d
        assert cols[name] % tn == 0 and width % tn == 0
        for t in range(cols[name] // tn, (cols[name] + width) // tn):
            out[t] = ty
    return jnp.asarray(out, jnp.int32)


def kernel(x, c, ctx, c_ctx, w_ada, b_ada, g_pre, g_post, w_in, sink, lam_qk, g_subln,
           w_proj_a, w_proj_b, w_out):
    batch, seq, d = x.shape
    ctx_len = ctx.shape[1]
    depth = w_in.shape[0]
    t_lat, t_ctx = batch * seq, batch * ctx_len
    cols = _col_offsets(d)
    kv_cols, in_cols = cols["kv_cols"], cols["in_cols"]

    tn_in = min(512, d // A_GROUP)
    tm_lat = _pick(seq, (1024, 512, 256))
    tm_in = _pick(seq, (2048, 1024, 512, 256))
    tm_ctx = _pick(t_ctx, (512, 256))
    tq_a = _pick(seq, (2048, 1024, 512, 256, 128))
    tq_b = _pick(seq, (1024, 512, 256))
    tk_b = _pick(seq // 4, (512, 256))
    tm_m = _pick(seq, (512, 256))

    cosf, sinf = _rope_tables(seq)
    zeros_tab = jnp.zeros((tm_ctx, HEAD_DIM), F32)
    types_lat = _tile_types(d, tn_in, rope=True)
    types_ctx = _tile_types(d, tn_in, rope=False)
    cvec8 = jnp.concatenate([c, c_ctx[None, :], jnp.zeros((8 - batch - 1, d), F32)], axis=0)

    x2 = x.reshape(t_lat, d)
    c2 = ctx.reshape(t_ctx, d)
    lat_row = lambda tm: (lambda i: i // (seq // tm))
    ctx_row = lambda i: batch

    mods = [_ada(cvec8, w_ada, b_ada, l).reshape(8, 1, 3 * d) for l in range(depth)]
    hx = _prenorm(x2, g_pre[0], mods[0], lat_row(tm_lat), tm_lat)
    hc = _prenorm(c2, g_pre[0], mods[0], ctx_row, tm_ctx)

    for l in range(depth):
        last = l == depth - 1
        lam_init = 0.8 - 0.6 * math.exp(-0.3 * l)
        wpa, wpb, wo = w_proj_a[l].astype(BF16), w_proj_b[l].astype(BF16), w_out[l].astype(BF16)
        mod3 = mods[l]
        pxl = _inproj(hx, w_in, l, types_lat, cosf, sinf, ncols=in_cols,
                      tm=tm_in, tn=tn_in, pos_tiles=seq // tm_in)
        pxc = _inproj(hc, w_in, l, types_ctx, zeros_tab, zeros_tab, ncols=kv_cols if last else in_cols,
                      tm=tm_ctx, tn=tn_in, pos_tiles=1)

        o_a = _attn_a(pxl, pxc, sink[l], batch=batch, seq=seq, ctx_len=ctx_len, d=d, tq=tq_a)
        o_b = _attn_b_latent(pxl, pxc, lam_qk[l], g_subln[l], lam_init=lam_init, batch=batch,
                             seq=seq, ctx_len=ctx_len, d=d, tq=tq_b, tk=tk_b)
        u = _merge1(o_a, o_b, pxl, wpa, wpb, d=d, tm=tm_m)

        if not last:
            oc_a = _attn_a_ctx(pxc, sink[l], batch=batch, ctx_len=ctx_len, d=d)
            oc_b = _attn_b_ctx(pxc, lam_qk[l], g_subln[l], lam_init=lam_init, batch=batch,
                               ctx_len=ctx_len, d=d)
            uc = _merge1(oc_a, oc_b, pxc, wpa, wpb, d=d, tm=tm_ctx)
            c2, hc = _merge2(uc, wo, c2, g_post[l], mod3, ctx_row, tm_ctx,
                             g_pre_next=g_pre[l + 1], mod3_next=mods[l + 1])
            x2, hx = _merge2(u, wo, x2, g_post[l], mod3, lat_row(tm_m), tm_m,
                             g_pre_next=g_pre[l + 1], mod3_next=mods[l + 1])
        else:
            x2 = _merge2(u, wo, x2, g_post[l], mod3, lat_row(tm_m), tm_m)

    return x2.reshape(batch, seq, d)
```
